```python
import math
import jax
import jax.numpy as jnp
from jax import lax
import numpy as np

D_MODEL = 4096
BATCH = 1
SEQ = 16384
DEPTH = 4

CTX_LEN = 256
GRID_W = 64
N_MIXERS = 3
N_LAYERS_A = len(range(0, DEPTH, N_MIXERS))
N_LAYERS_B = len(range(1, DEPTH, N_MIXERS))
N_LAYERS_C = len(range(2, DEPTH, N_MIXERS))
EPS = 1e-6
ADA_RANK = 256
N_MOD = 6
CONV_W = 4
D_RNN = D_MODEL
RG_HEADS = 16
RG_BW = D_RNN // RG_HEADS
RG_C = 8.0
D_INNER = 2 * D_MODEL
SSD_HEADDIM = 64
SSD_HEADS = D_INNER // SSD_HEADDIM
SSD_STATE = 128
SSD_GROUPS = 8
SSD_HG = SSD_HEADS // SSD_GROUPS
SSD_CHUNK = 128
D_CONV_SSD = D_INNER + 2 * SSD_GROUPS * SSD_STATE
D_IN_SSD = D_INNER + D_CONV_SSD + 2 * SSD_HEADS
HEAD_DIM = 128
ATTN_HEADS = D_MODEL // HEAD_DIM
ATTN_KV_HEADS = ATTN_HEADS // 4
ATTN_GROUP = ATTN_HEADS // ATTN_KV_HEADS
WINDOW = 128
ATTN_BLOCK = 128
ROPE_BASE = 10000.0
N_EXPERTS = 16
EXPERT_FF = 640
EC_CAPACITY = 2

kernel_name = 'hybrid_rglru_ssd_swa_ecmoe_diffusion'


def rms_norm(x, g):
    xf = x.astype(jnp.float32)
    y = xf * lax.rsqrt(jnp.mean(xf * xf, axis=-1, keepdims=True) + EPS)
    return (y * g.astype(jnp.float32)).astype(x.dtype)


def modulate(h, shift, scale):
    return h * (1 + scale) + shift


def adaln(cond, w_down, w_up, b):
    mod = ((cond @ w_down) @ w_up + b).reshape(cond.shape[0], N_MOD, D_MODEL)
    return jnp.moveaxis(mod, 1, 0)[:, :, None, :]


def dw_conv_centred(x, w, b):
    T = x.shape[1]
    left = CONV_W // 2
    xp = jnp.pad(x, ((0, 0), (left, CONV_W - 1 - left), (0, 0)))
    out = b
    for k in range(CONV_W):
        out = out + xp[:, k:k + T] * w[k]
    return out


def linear_scan(a, u, h0, reverse):
    def combine(left, right):
        a_l, b_l = left
        a_r, b_r = right
        return a_r * a_l, a_r * b_l + b_r
    a_cum, h = lax.associative_scan(combine, (a, u), axis=1, reverse=reverse)
    return h + a_cum * h0[:, None, :]


def rglru_coeffs(v, gate_w, gate_b, lam):
    vh = v.reshape(v.shape[:2] + (RG_HEADS, RG_BW))
    gates = jnp.einsum('bthi,khij->kbthj', vh, gate_w.astype(jnp.float32)).reshape((2,) + v.shape)
    gates = gates + gate_b.astype(jnp.float32)[:, None, None, :]
    r = jax.nn.sigmoid(gates[0])
    i = jax.nn.sigmoid(gates[1])
    log_a = -RG_C * r * jax.nn.softplus(-lam.astype(jnp.float32))
    a = jnp.exp(log_a)
    u = jnp.sqrt(-jnp.expm1(2.0 * log_a)) * (i * v)
    return a, u


def rglru_mixer(h_lat, h_ctx, w_in, conv_w, conv_b, gate_w, gate_b, lam, w_out):
    def branches(h):
        g, v = jnp.split(h @ w_in, 2, axis=-1)
        return jax.nn.gelu(g), dw_conv_centred(v, conv_w, conv_b).astype(jnp.float32)
    g_c, v_c = branches(h_ctx)
    g_l, v_l = branches(h_lat)
    ys_c, ys_l = [], []
    for d, reverse in enumerate((False, True)):
        a_c, u_c = rglru_coeffs(v_c, gate_w[d], gate_b[d], lam[d])
        hc = linear_scan(a_c, u_c, jnp.zeros_like(v_c[:, 0]), reverse)
        h0 = hc[:, 0] if reverse else hc[:, -1]
        a_l, u_l = rglru_coeffs(v_l, gate_w[d], gate_b[d], lam[d])
        hl = linear_scan(a_l, u_l, h0, reverse)
        ys_c.append(hc)
        ys_l.append(hl)
    y_c = ((ys_c[0] + ys_c[1]).astype(h_ctx.dtype) * g_c) @ w_out
    y_l = ((ys_l[0] + ys_l[1]).astype(h_lat.dtype) * g_l) @ w_out
    return y_l, y_c


def ssd_scan(x, dt, A, Bm, Cm, h0):
    b, T = x.shape[:2]
    nc = T // SSD_CHUNK
    shp = (b, nc, SSD_CHUNK, SSD_GROUPS)
    xq = (x * dt[..., None]).reshape(shp + (SSD_HG, SSD_HEADDIM))
    a_cum = jnp.cumsum((dt * A).reshape(shp + (SSD_HG,)), axis=2)
    Bq = Bm.reshape(shp + (SSD_STATE,))
    Cq = Cm.reshape(shp + (SSD_STATE,))
    causal = jnp.tril(jnp.ones((SSD_CHUNK, SSD_CHUNK), bool))[:, :, None, None]
    seg = a_cum[:, :, :, None] - a_cum[:, :, None, :]
    decay = jnp.exp(jnp.where(causal, seg, -jnp.inf))
    cb = jnp.einsum('bclgn,bcsgn->bclsg', Cq, Bq)
    y_diag = jnp.einsum('bclsg,bclsgh,bcsghp->bclghp', cb, decay, xq)
    to_end = jnp.exp(a_cum[:, :, -1:] - a_cum)
    states = jnp.einsum('bclgn,bclgh,bclghp->bcghpn', Bq, to_end, xq)
    chunk_decay = jnp.exp(a_cum[:, :, -1])

    def step(h, inp):
        st, dec = inp
        return h * dec[..., None, None] + st, h
    h_last, h_in = lax.scan(step, h0, (jnp.moveaxis(states, 1, 0), jnp.moveaxis(chunk_decay, 1, 0)))
    h_in = jnp.moveaxis(h_in, 0, 1)
    y_off = jnp.einsum('bclgn,bcghpn,bclgh->bclghp', Cq, h_in, jnp.exp(a_cum))
    return (y_diag + y_off).reshape(b, T, SSD_HEADS, SSD_HEADDIM), h_last


def ssd_mixer(h_lat, h_ctx, w_in, conv_w, conv_b, dt_bias, a_log, d_skip, norm_g, w_out):
    gn = SSD_GROUPS * SSD_STATE

    def prep(h):
        b, T = h.shape[:2]
        z, xbc, dt_raw = jnp.split(h @ w_in, [D_INNER, D_INNER + D_CONV_SSD], axis=-1)
        xbc = jax.nn.silu(dw_conv_centred(xbc, conv_w, conv_b)).astype(jnp.float32)
        xs, Bm, Cm = jnp.split(xbc, [D_INNER, D_INNER + gn], axis=-1)
        return (z, xs.reshape(b, T, SSD_HEADS, SSD_HEADDIM),
                Bm.reshape(b, T, SSD_GROUPS, SSD_STATE), Cm.reshape(b, T, SSD_GROUPS, SSD_STATE),
                dt_raw.astype(jnp.float32).reshape(b, T, 2, SSD_HEADS))

    def direction(p, d, h0, reverse):
        _, xs, Bm, Cm, dt_raw = p
        dt = jax.nn.softplus(dt_raw[:, :, d] + dt_bias[d].astype(jnp.float32))
        A = -jnp.exp(a_log[d].astype(jnp.float32))
        if reverse:
            xs, dt, Bm, Cm = (jnp.flip(t, axis=1) for t in (xs, dt, Bm, Cm))
        y, h_last = ssd_scan(xs, dt, A, Bm, Cm, h0)
        if reverse:
            y = jnp.flip(y, axis=1)
        return y, h_last

    def finish(p, ys, h):
        z, xs = p[0], p[1]
        y = ys[0] + ys[1] + d_skip.astype(jnp.float32)[:, None] * xs
        y = y.reshape(z.shape).astype(h.dtype) * jax.nn.silu(z)
        return rms_norm(y, norm_g) @ w_out

    pc, pl = prep(h_ctx), prep(h_lat)
    b = h_lat.shape[0]
    ys_c, ys_l = [], []
    for d, reverse in enumerate((False, True)):
        h0 = jnp.zeros((b, SSD_GROUPS, SSD_HG, SSD_HEADDIM, SSD_STATE), jnp.float32)
        yc, hc = direction(pc, d, h0, reverse)
        yl, _ = direction(pl, d, hc, reverse)
        ys_c.append(yc)
        ys_l.append(yl)
    return finish(pl, ys_l, h_lat), finish(pc, ys_c, h_ctx)


def rope_2d(x, rows, cols):
    half = HEAD_DIM // 2
    inv_freq = ROPE_BASE ** (-jnp.arange(0, half, 2, dtype=jnp.float32) / half)

    def rotate(v, pos):
        ang = pos.astype(jnp.float32)[:, None] * inv_freq
        cos = jnp.cos(ang)[None, :, None, :]
        sin = jnp.sin(ang)[None, :, None, :]
        v1, v2 = jnp.split(v, 2, axis=-1)
        return jnp.concatenate([v1 * cos - v2 * sin, v1 * sin + v2 * cos], axis=-1)
    xf = x.astype(jnp.float32)
    xr, xc = jnp.split(xf, 2, axis=-1)
    return jnp.concatenate([rotate(xr, rows), rotate(xc, cols)], axis=-1).astype(x.dtype)


def window_attention(h_lat, h_ctx, rows, cols, w_qkv, sink, w_out):
    b, T, _ = h_lat.shape
    L = h_ctx.shape[1]
    scale = HEAD_DIM ** -0.5

    def project(h):
        n = h.shape[1]
        q, k, v = jnp.split(h @ w_qkv, [ATTN_HEADS * HEAD_DIM, (ATTN_HEADS + ATTN_KV_HEADS) * HEAD_DIM], axis=-1)
        return (q.reshape(b, n, ATTN_HEADS, HEAD_DIM), k.reshape(b, n, ATTN_KV_HEADS, HEAD_DIM),
                v.reshape(b, n, ATTN_KV_HEADS, HEAD_DIM))
    qc, kc, vc = project(h_ctx)
    ql, kl, vl = project(h_lat)
    ql, kl = rope_2d(ql, rows, cols), rope_2d(kl, rows, cols)
    sink_f = sink.astype(jnp.float32).reshape(ATTN_KV_HEADS, ATTN_GROUP)

    qc = qc.reshape(b, L, ATTN_KV_HEADS, ATTN_GROUP, HEAD_DIM)
    s_cc = jnp.einsum('bqhgd,bkhd->bhgqk', qc, kc).astype(jnp.float32) * scale
    sink_cc = jnp.broadcast_to(sink_f[None, :, :, None, None], s_cc.shape[:-1] + (1,))
    p_c = jax.nn.softmax(jnp.concatenate([s_cc, sink_cc], axis=-1), axis=-1)[..., :L]
    o_c = jnp.einsum('bhgqk,bkhd->bqhgd', p_c.astype(vc.dtype), vc).reshape(b, L, ATTN_HEADS * HEAD_DIM)

    nb = T // ATTN_BLOCK
    nk = 3 * ATTN_BLOCK
    qb = ql.reshape(b, nb, ATTN_BLOCK, ATTN_KV_HEADS, ATTN_GROUP, HEAD_DIM)

    def band(t):
        tp = jnp.pad(t, ((0, 0), (ATTN_BLOCK, ATTN_BLOCK), (0, 0), (0, 0)))
        tp = tp.reshape(b, nb + 2, ATTN_BLOCK, ATTN_KV_HEADS, HEAD_DIM)
        return jnp.concatenate([tp[:, :-2], tp[:, 1:-1], tp[:, 2:]], axis=2)
    kb, vb = band(kl), band(vl)
    s_ll = jnp.einsum('bnqhgd,bnkhd->bhgnqk', qb, kb).astype(jnp.float32) * scale
    q_idx = jnp.arange(ATTN_BLOCK)[:, None] + ATTN_BLOCK
    k_idx = jnp.arange(nk)[None, :]
    k_abs = jnp.arange(nb)[:, None, None] * ATTN_BLOCK - ATTN_BLOCK + k_idx
    valid = (jnp.abs(q_idx - k_idx) <= WINDOW) & (k_abs >= 0) & (k_abs < T)
    s_ll = jnp.where(valid, s_ll, -jnp.inf)
    s_lc = jnp.einsum('bnqhgd,bkhd->bhgnqk', qb, kc).astype(jnp.float32) * scale
    sink_l = jnp.broadcast_to(sink_f[None, :, :, None, None, None], s_ll.shape[:-1] + (1,))
    p_l = jax.nn.softmax(jnp.concatenate([s_ll, s_lc, sink_l], axis=-1), axis=-1).astype(vb.dtype)
    o_l = (jnp.einsum('bhgnqk,bnkhd->bnqhgd', p_l[..., :nk], vb)
           + jnp.einsum('bhgnqk,bkhd->bnqhgd', p_l[..., nk:nk + L], vc))
    o_l = o_l.reshape(b, T, ATTN_HEADS * HEAD_DIM)
    return o_l @ w_out, o_c @ w_out


def expert_choice_ffn(h, router_w, w_gate, w_up, w_down):
    b, T, d = h.shape
    cap = EC_CAPACITY * T // N_EXPERTS
    aff = jax.nn.softmax((h @ router_w).astype(jnp.float32), axis=-1)
    gate, idx = lax.top_k(jnp.swapaxes(aff, 1, 2), cap)
    xe = jax.vmap(lambda hb, ib: hb[ib])(h, idx)
    hid = jax.nn.silu(jnp.einsum('becd,edf->becf', xe, w_gate)) * jnp.einsum('becd,edf->becf', xe, w_up)
    ye = jnp.einsum('becf,efd->becd', hid, w_down) * gate[..., None].astype(h.dtype)
    return jax.vmap(lambda yb, ib: jnp.zeros((T, d), yb.dtype).at[ib.reshape(-1)].add(yb.reshape(-1, d)))(ye, idx)


def setup_inputs(seed: int = 0) -> dict:
    key = jax.random.key(seed)
    ks = iter(jax.random.split(key, 40))
    f32 = jnp.float32
    D = D_MODEL

    def nrm(shape, s):
        return jax.random.normal(next(ks), shape, f32) * s
    inp = {}
    inp['x'] = nrm((BATCH, SEQ, D), 1.0)
    inp['c'] = nrm((BATCH, D), 1.0)
    inp['ctx'] = nrm((BATCH, CTX_LEN, D), 1.0)
    inp['c_ctx'] = nrm((D,), 1.0)
    inp['ada_down'] = nrm((DEPTH, D, ADA_RANK), D ** -0.5)
    inp['ada_up'] = nrm((DEPTH, ADA_RANK, N_MOD * D), 0.3 * ADA_RANK ** -0.5)
    inp['ada_b'] = nrm((DEPTH, N_MOD * D), 0.02)
    inp['norm_g'] = 1.0 + nrm((DEPTH, 4, D), 0.05)
    inp['router_w'] = nrm((DEPTH, D, N_EXPERTS), D ** -0.5)
    inp['moe_w_gate'] = nrm((DEPTH, N_EXPERTS, D, EXPERT_FF), D ** -0.5)
    inp['moe_w_up'] = nrm((DEPTH, N_EXPERTS, D, EXPERT_FF), D ** -0.5)
    inp['moe_w_down'] = nrm((DEPTH, N_EXPERTS, EXPERT_FF, D), EXPERT_FF ** -0.5)
    inp['rg_w_in'] = nrm((N_LAYERS_A, D, 2 * D_RNN), D ** -0.5)
    inp['rg_conv_w'] = nrm((N_LAYERS_A, CONV_W, D_RNN), CONV_W ** -0.5)
    inp['rg_conv_b'] = nrm((N_LAYERS_A, D_RNN), 0.02)
    inp['rg_gate_w'] = nrm((N_LAYERS_A, 2, 2, RG_HEADS, RG_BW, RG_BW), RG_BW ** -0.5)
    inp['rg_gate_b'] = nrm((N_LAYERS_A, 2, 2, D_RNN), 0.1)
    a0 = jax.random.uniform(next(ks), (N_LAYERS_A, 2, D_RNN), f32, 0.9, 0.999) ** (1.0 / RG_C)
    inp['rg_lambda'] = jnp.log(a0) - jnp.log1p(-a0)
    inp['rg_w_out'] = nrm((N_LAYERS_A, D_RNN, D), D_RNN ** -0.5)
    inp['ssd_w_in'] = nrm((N_LAYERS_B, D, D_IN_SSD), D ** -0.5)
    inp['ssd_conv_w'] = nrm((N_LAYERS_B, CONV_W, D_CONV_SSD), CONV_W ** -0.5)
    inp['ssd_conv_b'] = nrm((N_LAYERS_B, D_CONV_SSD), 0.02)
    dt0 = jnp.exp(jax.random.uniform(next(ks), (N_LAYERS_B, 2, SSD_HEADS), f32, math.log(1e-3), math.log(1e-1)))
    inp['ssd_dt_bias'] = dt0 + jnp.log(-jnp.expm1(-dt0))
    inp['ssd_a_log'] = jnp.log(jax.random.uniform(next(ks), (N_LAYERS_B, 2, SSD_HEADS), f32, 1.0, 16.0))
    inp['ssd_d'] = 1.0 + nrm((N_LAYERS_B, SSD_HEADS), 0.1)
    inp['ssd_norm_g'] = 1.0 + nrm((N_LAYERS_B, D_INNER), 0.05)
    inp['ssd_w_out'] = nrm((N_LAYERS_B, D_INNER, D), D_INNER ** -0.5)
    inp['attn_w_qkv'] = nrm((N_LAYERS_C, D, (ATTN_HEADS + 2 * ATTN_KV_HEADS) * HEAD_DIM), D ** -0.5)
    inp['attn_sink'] = nrm((N_LAYERS_C, ATTN_HEADS), 0.5)
    inp['attn_w_out'] = nrm((N_LAYERS_C, ATTN_HEADS * HEAD_DIM, D), (ATTN_HEADS * HEAD_DIM) ** -0.5)
    return inp


def reference(x, c, ctx, c_ctx, ada_down, ada_up, ada_b, norm_g, router_w, moe_w_gate, moe_w_up, moe_w_down,
              rg_w_in, rg_conv_w, rg_conv_b, rg_gate_w, rg_gate_b, rg_lambda, rg_w_out,
              ssd_w_in, ssd_conv_w, ssd_conv_b, ssd_dt_bias, ssd_a_log, ssd_d, ssd_norm_g, ssd_w_out,
              attn_w_qkv, attn_sink, attn_w_out):
    T = x.shape[1]
    ROWS = T // GRID_W
    rows = jnp.repeat(jnp.arange(ROWS), GRID_W)
    cols = jnp.tile(jnp.arange(GRID_W), ROWS)
    cond_lat = jax.nn.silu(c)
    cond_ctx = jax.nn.silu(c_ctx)[None, :]
    xl, xc = x, ctx
    for i in range(DEPTH):
        kind, slot = i % N_MIXERS, i // N_MIXERS
        ml = adaln(cond_lat, ada_down[i], ada_up[i], ada_b[i])
        mc = adaln(cond_ctx, ada_down[i], ada_up[i], ada_b[i])
        h_l = modulate(rms_norm(xl, norm_g[i, 0]), ml[0], ml[1])
        h_c = modulate(rms_norm(xc, norm_g[i, 0]), mc[0], mc[1])
        if kind == 0:
            y_l, y_c = rglru_mixer(h_l, h_c, rg_w_in[slot], rg_conv_w[slot], rg_conv_b[slot], rg_gate_w[slot],
                                   rg_gate_b[slot], rg_lambda[slot], rg_w_out[slot])
        elif kind == 1:
            y_l, y_c = ssd_mixer(h_l, h_c, ssd_w_in[slot], ssd_conv_w[slot], ssd_conv_b[slot], ssd_dt_bias[slot],
                                 ssd_a_log[slot], ssd_d[slot], ssd_norm_g[slot], ssd_w_out[slot])
        else:
            y_l, y_c = window_attention(h_l, h_c, rows, cols, attn_w_qkv[slot], attn_sink[slot], attn_w_out[slot])
        xl = xl + ml[2] * rms_norm(y_l, norm_g[i, 1])
        f_l = expert_choice_ffn(modulate(rms_norm(xl, norm_g[i, 2]), ml[3], ml[4]),
                                router_w[i], moe_w_gate[i], moe_w_up[i], moe_w_down[i])
        xl = xl + ml[5] * rms_norm(f_l, norm_g[i, 3])
        if i < DEPTH - 1:
            xc = xc + mc[2] * rms_norm(y_c, norm_g[i, 1])
            f_c = expert_choice_ffn(modulate(rms_norm(xc, norm_g[i, 2]), mc[3], mc[4]),
                                    router_w[i], moe_w_gate[i], moe_w_up[i], moe_w_down[i])
            xc = xc + mc[5] * rms_norm(f_c, norm_g[i, 3])
    return xl
```

```python
import functools
import math

import jax
import jax.numpy as jnp
from jax import lax
from jax.experimental import pallas as pl
from jax.experimental.pallas import tpu as pltpu

f32 = jnp.float32
bf16 = jnp.bfloat16
i32 = jnp.int32

EPS = 1e-6
N_MIXERS = 3
N_MOD = 6
CONV_W = 4
RG_HEADS = 16
RG_C = 8.0
SSD_HEADDIM = 64
SSD_STATE = 128
SSD_GROUPS = 8
SSD_CHUNK = 128
HEAD_DIM = 128
ATTN_GROUP = 4
WINDOW = 128
ROPE_BASE = 10000.0
GRID_W = 64
N_EXPERTS = 16
EC_CAPACITY = 2

LANES = 128
SUBLANES = 8
ROW_SPLIT = 8
VMEM_LIMIT = 56 * 1024 * 1024


def _cparams(*sem):
    return pltpu.CompilerParams(dimension_semantics=sem, vmem_limit_bytes=VMEM_LIMIT)


def _row_tile(m, want):
    t = min(m, want)
    assert m % t == 0
    return t


def _rms_mod(x, g, shift, scale):
    y = x * lax.rsqrt(jnp.mean(x * x, axis=-1, keepdims=True) + EPS)
    return (y * g) * (1.0 + scale) + shift


def _nmm_kernel(x_ref, g_ref, sh_ref, sc_ref, w_ref, o_ref, a_ref):
    @pl.when(pl.program_id(1) == 0)
    def _():
        a_ref[...] = _rms_mod(x_ref[...], g_ref[...], sh_ref[...], sc_ref[...]).astype(a_ref.dtype)

    o_ref[...] = jnp.dot(a_ref[...], w_ref[...], preferred_element_type=f32).astype(o_ref.dtype)


def norm_mod_matmul(x, g, shift, scale, w, *, tm=512, tn=1024, out_dtype=f32):
    m, d = x.shape
    n = w.shape[1]
    tm = _row_tile(m, tm)
    tn = _row_tile(n, tn)
    vec = pl.BlockSpec((1, d), lambda i, j: (0, 0))
    return pl.pallas_call(
        _nmm_kernel,
        grid=(m // tm, n // tn),
        in_specs=[pl.BlockSpec((tm, d), lambda i, j: (i, 0)), vec, vec, vec,
                  pl.BlockSpec((d, tn), lambda i, j: (0, j))],
        out_specs=pl.BlockSpec((tm, tn), lambda i, j: (i, j)),
        out_shape=jax.ShapeDtypeStruct((m, n), out_dtype),
        scratch_shapes=[pltpu.VMEM((tm, d), bf16)],
        compiler_params=_cparams("parallel", "arbitrary"),
        name="norm_mod_matmul",
    )(x, g, shift, scale, w)


def _mnr_kernel(a_ref, w_ref, g_ref, gate_ref, r_ref, o_ref, y_ref, ss_ref, *, nj, n_total):
    j = pl.program_id(1)

    @pl.when(j == 0)
    def _():
        ss_ref[...] = jnp.zeros_like(ss_ref)

    @pl.when(j < nj)
    def _():
        y = jnp.dot(a_ref[...], w_ref[...], preferred_element_type=f32)
        y_ref[j] = y
        ss_ref[...] += jnp.sum(y * y, axis=-1, keepdims=True)

    @pl.when(j >= nj)
    def _():
        rinv = lax.rsqrt(ss_ref[...] * (1.0 / n_total) + EPS)
        o_ref[...] = r_ref[...] + gate_ref[...] * ((y_ref[j - nj] * rinv) * g_ref[...])


def matmul_norm_res(a, w, g, gate, resid, *, tm=512, tn=512):
    m, k = a.shape
    n = w.shape[1]
    tm = _row_tile(m, tm)
    tn = _row_tile(n, tn)
    nj = n // tn
    ph2 = lambda j: jnp.maximum(j - nj, 0)
    return pl.pallas_call(
        functools.partial(_mnr_kernel, nj=nj, n_total=n),
        grid=(m // tm, 2 * nj),
        in_specs=[pl.BlockSpec((tm, k), lambda i, j: (i, 0)),
                  pl.BlockSpec((k, tn), lambda i, j: (0, jnp.minimum(j, nj - 1))),
                  pl.BlockSpec((1, tn), lambda i, j: (0, ph2(j))),
                  pl.BlockSpec((1, tn), lambda i, j: (0, ph2(j))),
                  pl.BlockSpec((tm, tn), lambda i, j: (i, ph2(j)))],
        out_specs=pl.BlockSpec((tm, tn), lambda i, j: (i, ph2(j))),
        out_shape=jax.ShapeDtypeStruct((m, n), f32),
        scratch_shapes=[pltpu.VMEM((nj, tm, tn), f32), pltpu.VMEM((tm, 1), f32)],
        compiler_params=_cparams("parallel", "arbitrary"),
        name="matmul_norm_res",
    )(a, w, g, gate, resid)


def _softplus(x):
    return jnp.maximum(x, 0.0) + jnp.log1p(jnp.exp(-jnp.abs(x)))


def _rg_sweep_kernel(*refs, reverse, final, nt, tt):
    if final:
        (v_ref, vp_ref, vn_ref, cw_ref, cb_ref, gw_ref, gb_ref, lam_ref, h0_ref, hf_ref, g_ref,
         out_ref, hT_ref, ext_s, a_s, u_s, hs_s, carry_s) = refs
    else:
        (v_ref, vp_ref, vn_ref, cw_ref, cb_ref, gw_ref, gb_ref, lam_ref, h0_ref,
         out_ref, hT_ref, ext_s, a_s, u_s, hs_s, carry_s) = refs
    t = pl.program_id(1)
    ti = nt - 1 - t if reverse else t
    c = v_ref.shape[1]
    s_len = tt // SUBLANES

    @pl.when(t == 0)
    def _():
        carry_s[...] = h0_ref[...]

    ext_s[0:8, :] = jnp.where(ti == 0, 0.0, vp_ref[...])
    ext_s[8:8 + tt, :] = v_ref[...]
    ext_s[8 + tt:16 + tt, :] = jnp.where(ti == nt - 1, 0.0, vn_ref[...])
    vc = cb_ref[...]
    for k in range(CONV_W):
        vc = vc + ext_s[pl.ds(6 + k, tt), :] * cw_ref[k:k + 1, :]

    vb = vc.astype(bf16)
    gr = jnp.dot(vb, gw_ref[0], preferred_element_type=f32) + gb_ref[0:1, :]
    gi = jnp.dot(vb, gw_ref[1], preferred_element_type=f32) + gb_ref[1:2, :]
    r = jax.nn.sigmoid(gr)
    ig = jax.nn.sigmoid(gi)
    log_a = (-RG_C) * r * _softplus(-lam_ref[...])
    a = jnp.exp(log_a)
    u = jnp.sqrt(-jnp.tanh(log_a) * (a * a + 1.0)) * (ig * vc)

    a_s[...] = a.reshape(SUBLANES, s_len, c)
    u_s[...] = u.reshape(SUBLANES, s_len, c)

    def jj(j):
        return s_len - 1 - j if reverse else j

    def pass1(j, hp):
        h, p = hp
        av = a_s[:, jj(j), :]
        return av * h + u_s[:, jj(j), :], av * p

    hfin, pfin = lax.fori_loop(0, s_len, pass1, (jnp.zeros((SUBLANES, c), f32), jnp.ones((SUBLANES, c), f32)),
                               unroll=8)
    cur = carry_s[...]
    cins = [None] * SUBLANES
    for s in (range(SUBLANES - 1, -1, -1) if reverse else range(SUBLANES)):
        cins[s] = cur
        cur = hfin[s:s + 1, :] + pfin[s:s + 1, :] * cur
    carry_s[...] = cur
    hT_ref[...] = cur

    def pass2(j, h):
        h = a_s[:, jj(j), :] * h + u_s[:, jj(j), :]
        hs_s[:, jj(j), :] = h
        return h

    lax.fori_loop(0, s_len, pass2, jnp.concatenate(cins, axis=0), unroll=8)
    hseq = hs_s[...].reshape(tt, c)
    if final:
        out_ref[...] = ((hf_ref[...] + hseq) * jax.nn.gelu(g_ref[...])).astype(out_ref.dtype)
    else:
        out_ref[...] = hseq


def rg_sweep(gv, conv_w, conv_b, gate_w, gate_b, lam, h0, hf=None, *, reverse, tt=512):
    T, d2 = gv.shape
    d = d2 // 2
    c = d // RG_HEADS
    nh = RG_HEADS
    tt = _row_tile(T, tt)
    nt = T // tt
    final = hf is not None
    tb = tt // 8
    tix = (lambda t: nt - 1 - t) if reverse else (lambda t: t)
    head_vec = lambda rows: pl.BlockSpec((rows, c), lambda h, t: (0, h))
    in_specs = [
        pl.BlockSpec((tt, c), lambda h, t: (tix(t), nh + h)),
        pl.BlockSpec((8, c), lambda h, t: (jnp.maximum(tix(t) * tb - 1, 0), nh + h)),
        pl.BlockSpec((8, c), lambda h, t: (jnp.minimum((tix(t) + 1) * tb, T // 8 - 1), nh + h)),
        head_vec(CONV_W), head_vec(1),
        pl.BlockSpec((2, None, c, c), lambda h, t: (0, h, 0, 0)),
        head_vec(2), head_vec(1), head_vec(1),
    ]
    args = [gv, gv, gv, conv_w, conv_b, gate_w, gate_b, lam, h0]
    if final:
        in_specs += [pl.BlockSpec((tt, c), lambda h, t: (tix(t), h)),
                     pl.BlockSpec((tt, c), lambda h, t: (tix(t), h))]
        args += [hf, gv]
    return pl.pallas_call(
        functools.partial(_rg_sweep_kernel, reverse=reverse, final=final, nt=nt, tt=tt),
        grid=(nh, nt),
        in_specs=in_specs,
        out_specs=[pl.BlockSpec((tt, c), lambda h, t: (tix(t), h)), head_vec(1)],
        out_shape=[jax.ShapeDtypeStruct((T, d), bf16 if final else f32), jax.ShapeDtypeStruct((1, d), f32)],
        scratch_shapes=[pltpu.VMEM((tt + 16, c), f32)] + [pltpu.VMEM((SUBLANES, tt // SUBLANES, c), f32)] * 3
        + [pltpu.VMEM((1, c), f32)],
        compiler_params=_cparams("parallel", "arbitrary"),
        name="rg_sweep_bwd" if reverse else "rg_sweep_fwd",
    )(*args)


def rglru_mixer(h_args_l, h_args_c, p):
    w_in = p["w_in"]
    zeros = jnp.zeros((1, w_in.shape[0]), f32)
    gv_c = norm_mod_matmul(*h_args_c, w_in)
    gv_l = norm_mod_matmul(*h_args_l, w_in)
    sw = lambda gv, d, h0, hf, rev: rg_sweep(gv, p["conv_w"], p["conv_b"], p["gate_w"][d], p["gate_b"][d],
                                             p["lam"][d], h0, hf, reverse=rev)
    hf_c, s_c = sw(gv_c, 0, zeros, None, False)
    hf_l, _ = sw(gv_l, 0, s_c, None, False)
    y_c, s_c = sw(gv_c, 1, zeros, hf_c, True)
    y_l, _ = sw(gv_l, 1, s_c, hf_l, True)
    return y_l, y_c


def _router_kernel(x_ref, g_ref, sh_ref, sc_ref, rw_ref, h_ref, aff_ref):
    h = _rms_mod(x_ref[...], g_ref[...], sh_ref[...], sc_ref[...])
    w = h.shape[1] // ROW_SPLIT
    for s in range(ROW_SPLIT):
        h_ref[:, s, :] = h[:, s * w:(s + 1) * w]
    logits = lax.dot_general(rw_ref[...], h, (((1,), (1,)), ((), ())), precision=lax.Precision.HIGHEST,
                             preferred_element_type=f32)
    e = jnp.exp(logits - jnp.max(logits, axis=0, keepdims=True))
    aff_ref[...] = e / jnp.sum(e, axis=0, keepdims=True)


def moe_router(x, g, shift, scale, router_wt, *, tm=512):
    m, d = x.shape
    ne = router_wt.shape[0]
    tm = _row_tile(m, tm)
    vec = pl.BlockSpec((1, d), lambda i: (0, 0))
    return pl.pallas_call(
        _router_kernel,
        grid=(m // tm,),
        in_specs=[pl.BlockSpec((tm, d), lambda i: (i, 0)), vec, vec, vec,
                  pl.BlockSpec((ne, d), lambda i: (0, 0))],
        out_specs=[pl.BlockSpec((tm, ROW_SPLIT, d // ROW_SPLIT), lambda i: (i, 0, 0)),
                   pl.BlockSpec((ne, tm), lambda i: (0, i))],
        out_shape=[jax.ShapeDtypeStruct((m, ROW_SPLIT, d // ROW_SPLIT), f32),
                   jax.ShapeDtypeStruct((ne, m), f32)],
        compiler_params=_cparams("parallel"),
        name="moe_router",
    )(x, g, shift, scale, router_wt)


def _select_kernel(aff_ref, idx_ref, gate_ref, cnt_s, *, cap, jb):
    nb = aff_ref.shape[0]
    aff = aff_ref[...]
    keys = pltpu.bitcast(aff, i32)

    def total(m):
        return jnp.sum(jnp.sum(m.astype(i32), axis=0, keepdims=True), axis=1, keepdims=True)

    def bs(i, thr):
        cand = thr | (jnp.int32(1) << (30 - i))
        return jnp.where(total(keys >= cand) >= cap, cand, thr)

    thr = lax.fori_loop(0, 31, bs, jnp.zeros((1, 1), i32))
    gt = keys > thr
    eq = keys == thr
    need = (cap - total(gt)).astype(f32)

    tri = (lax.broadcasted_iota(i32, (LANES, LANES), 0) <= lax.broadcasted_iota(i32, (LANES, LANES), 1)).astype(bf16)
    low = (lax.broadcasted_iota(i32, (nb, nb), 1) < lax.broadcasted_iota(i32, (nb, nb), 0)).astype(f32)

    def running_count(m):
        within = jnp.dot(m.astype(f32).astype(bf16), tri, preferred_element_type=f32)
        rows = jnp.broadcast_to(within[:, LANES - 1:LANES], (nb, LANES))
        return within + jnp.dot(low, rows, precision=lax.Precision.HIGHEST, preferred_element_type=f32)

    sel = gt | (eq & (running_count(eq) - eq.astype(f32) < need))
    cnt_s[...] = jnp.where(sel, running_count(sel), 0.0)

    lane = lax.broadcasted_iota(i32, (jb, LANES), 1).astype(f32)
    row1 = lax.broadcasted_iota(i32, (jb, LANES), 0).astype(f32) + 1.0

    def jloop(jc, carry):
        slot = row1 + lax.convert_element_type(jc * jb, f32)

        def tloop(tb, acc):
            ai, ag = acc
            hit = cnt_s[pl.ds(tb, 1), :] == slot
            ai = ai + jnp.where(hit, lane + lax.convert_element_type(tb * LANES, f32), 0.0)
            ag = ag + jnp.where(hit, aff_ref[pl.ds(tb, 1), :], 0.0)
            return ai, ag

        ai, ag = lax.fori_loop(0, nb, tloop, (jnp.zeros((jb, LANES), f32), jnp.zeros((jb, LANES), f32)))
        off = jc * jb
        idx_ref[pl.ds(off, jb), :] = jnp.sum(ai, axis=1, keepdims=True).astype(i32)
        gate_ref[pl.ds(off, jb), :] = jnp.sum(ag, axis=1, keepdims=True)
        return carry

    lax.fori_loop(0, cap // jb, jloop, 0)


def moe_select(aff_t, cap):
    ne, T = aff_t.shape
    jb = min(cap, 128)
    tile = SUBLANES * LANES
    tp = -(-T // tile) * tile
    aff3 = jnp.pad(aff_t, ((0, 0), (0, tp - T))).reshape(ne, tp // LANES, LANES)
    nb = tp // LANES
    out = pl.BlockSpec((None, cap, 1), lambda e: (e, 0, 0))
    return pl.pallas_call(
        functools.partial(_select_kernel, cap=cap, jb=jb),
        grid=(ne,),
        in_specs=[pl.BlockSpec((None, nb, LANES), lambda e: (e, 0, 0))],
        out_specs=[out, out],
        out_shape=[jax.ShapeDtypeStruct((ne, cap, 1), i32), jax.ShapeDtypeStruct((ne, cap, 1), f32)],
        scratch_shapes=[pltpu.VMEM((nb, LANES), f32)],
        compiler_params=_cparams("parallel"),
        name="moe_select",
    )(aff3)


def _expert_kernel(idx_ref, gate_ref, h_hbm, acc_hbm, wg_ref, wu_ref, wd_ref, out_hbm, xbuf, abuf, sem, *, rows):
    e = pl.program_id(0)
    base = pl.program_id(1) * rows
    w = xbuf.shape[2]

    def gather(r, c):
        t = idx_ref[e, base + r]
        pltpu.make_async_copy(h_hbm.at[t], xbuf.at[r], sem.at[0]).start()
        pltpu.make_async_copy(acc_hbm.at[t], abuf.at[r], sem.at[1]).start()
        return c

    lax.fori_loop(0, rows, gather, 0)
    pltpu.make_async_copy(h_hbm.at[pl.ds(0, rows)], xbuf, sem.at[0]).wait()
    hg = None
    hu = None
    for s in range(ROW_SPLIT):
        xs = xbuf[:, s, :].astype(bf16)
        pg = jnp.dot(xs, wg_ref[s * w:(s + 1) * w, :], preferred_element_type=f32)
        pu = jnp.dot(xs, wu_ref[s * w:(s + 1) * w, :], preferred_element_type=f32)
        hg = pg if hg is None else hg + pg
        hu = pu if hu is None else hu + pu
    hid = (jax.nn.silu(hg) * hu).astype(bf16)
    gate = gate_ref[...]
    pltpu.make_async_copy(acc_hbm.at[pl.ds(0, rows)], abuf, sem.at[1]).wait()
    for s in range(ROW_SPLIT):
        y = jnp.dot(hid, wd_ref[:, s * w:(s + 1) * w], preferred_element_type=f32)
        abuf[:, s, :] = abuf[:, s, :] + y * gate

    def scatter(r, c):
        t = idx_ref[e, base + r]
        pltpu.make_async_copy(abuf.at[r], out_hbm.at[t], sem.at[2]).start()
        return c

    lax.fori_loop(0, rows, scatter, 0)
    pltpu.make_async_copy(abuf, out_hbm.at[pl.ds(0, rows)], sem.at[2]).wait()


def moe_experts(idx, gate, h3, acc3, w_gate, w_up, w_down, *, rows=256):
    ne, cap = idx.shape
    T, rs, w = h3.shape
    d = rs * w
    ff = w_gate.shape[2]
    rows = _row_tile(cap, rows)
    return pl.pallas_call(
        functools.partial(_expert_kernel, rows=rows),
        grid_spec=pltpu.PrefetchScalarGridSpec(
            num_scalar_prefetch=1,
            grid=(ne, cap // rows),
            in_specs=[pl.BlockSpec((None, rows, 1), lambda e, j, idx: (e, j, 0)),
                      pl.BlockSpec(memory_space=pl.ANY),
                      pl.BlockSpec(memory_space=pl.ANY),
                      pl.BlockSpec((None, d, ff), lambda e, j, idx: (e, 0, 0)),
                      pl.BlockSpec((None, d, ff), lambda e, j, idx: (e, 0, 0)),
                      pl.BlockSpec((None, ff, d), lambda e, j, idx: (e, 0, 0))],
            out_specs=pl.BlockSpec(memory_space=pl.ANY),
            scratch_shapes=[pltpu.VMEM((rows, rs, w), f32), pltpu.VMEM((rows, rs, w), f32),
                            pltpu.SemaphoreType.DMA((3,))]),
        out_shape=jax.ShapeDtypeStruct((T, rs, w), f32),
        input_output_aliases={3: 0},
        compiler_params=_cparams("arbitrary", "arbitrary"),
        name="moe_experts",
    )(idx, gate, h3, acc3, w_gate, w_up, w_down)


def _moe_res_kernel(x_ref, f_ref, g_ref, gate_ref, o_ref):
    w = f_ref.shape[2]
    parts = [f_ref[:, s, :] for s in range(ROW_SPLIT)]
    ss = sum(jnp.sum(p * p, axis=-1, keepdims=True) for p in parts)
    rinv = lax.rsqrt(ss * (1.0 / (w * ROW_SPLIT)) + EPS)
    for s in range(ROW_SPLIT):
        sl = slice(s * w, (s + 1) * w)
        o_ref[:, sl] = x_ref[:, sl] + gate_ref[:, sl] * ((parts[s] * rinv) * g_ref[:, sl])


def moe_residual(x, f3, g, gate, *, tm=512):
    m, d = x.shape
    tm = _row_tile(m, tm)
    vec = pl.BlockSpec((1, d), lambda i: (0, 0))
    return pl.pallas_call(
        _moe_res_kernel,
        grid=(m // tm,),
        in_specs=[pl.BlockSpec((tm, d), lambda i: (i, 0)),
                  pl.BlockSpec((tm, ROW_SPLIT, d // ROW_SPLIT), lambda i: (i, 0, 0)), vec, vec],
        out_specs=pl.BlockSpec((tm, d), lambda i: (i, 0)),
        out_shape=jax.ShapeDtypeStruct((m, d), f32),
        compiler_params=_cparams("parallel"),
        name="moe_residual",
    )(x, f3, g, gate)


def moe_block(x, g_in, shift, scale, g_out, gate, p):
    T = x.shape[0]
    cap = EC_CAPACITY * T // N_EXPERTS
    h3, aff_t = moe_router(x, g_in, shift, scale, p["router_wt"])
    idx, gates = moe_select(aff_t, cap)
    f3 = moe_experts(idx.reshape(N_EXPERTS, cap), gates, h3, jnp.zeros_like(h3), p["w_gate"], p["w_up"], p["w_down"],
                     rows=min(cap, 256))
    return moe_residual(x, f3, g_out, gate)


def _conv_silu_kernel(x_ref, xp_ref, xn_ref, cw_ref, cb_ref, o_ref, ext_s, *, nt, tt):
    ti = pl.program_id(0)
    ext_s[0:8, :] = jnp.where(ti == 0, 0.0, xp_ref[...])
    ext_s[8:8 + tt, :] = x_ref[...]
    ext_s[8 + tt:16 + tt, :] = jnp.where(ti == nt - 1, 0.0, xn_ref[...])
    acc = cb_ref[...]
    for k in range(CONV_W):
        acc = acc + ext_s[pl.ds(6 + k, tt), :] * cw_ref[k:k + 1, :]
    o_ref[...] = jax.nn.silu(acc)


def conv_silu(zx, conv_w, conv_b, col0, *, tt=512, tc=1024):
    T = zx.shape[0]
    n = conv_w.shape[1]
    tt = _row_tile(T, tt)
    nt = T // tt
    tb = tt // 8
    cb0 = col0 // tc
    return pl.pallas_call(
        functools.partial(_conv_silu_kernel, nt=nt, tt=tt),
        grid=(nt, n // tc),
        in_specs=[pl.BlockSpec((tt, tc), lambda t, j: (t, cb0 + j)),
                  pl.BlockSpec((8, tc), lambda t, j: (jnp.maximum(t * tb - 1, 0), cb0 + j)),
                  pl.BlockSpec((8, tc), lambda t, j: (jnp.minimum((t + 1) * tb, T // 8 - 1), cb0 + j)),
                  pl.BlockSpec((CONV_W, tc), lambda t, j: (0, j)),
                  pl.BlockSpec((1, tc), lambda t, j: (0, j))],
        out_specs=pl.BlockSpec((tt, tc), lambda t, j: (t, j)),
        out_shape=jax.ShapeDtypeStruct((T, n), f32),
        scratch_shapes=[pltpu.VMEM((tt + 16, tc), f32)],
        compiler_params=_cparams("parallel", "parallel"),
        name="conv_silu",
    )(zx, zx, zx, conv_w, conv_b)


def _ssd_prep_kernel(raw_ref, bias_ref, alog_ref, dt_ref, ac_ref, act_ref):
    L = SSD_CHUNK
    nh = raw_ref.shape[1] // 2
    hg = nh // SSD_GROUPS
    dt = _softplus(raw_ref[...] + bias_ref[...])
    dta = dt * (-jnp.exp(alog_ref[...]))
    r = lax.broadcasted_iota(i32, (L, L), 0)
    c = lax.broadcasted_iota(i32, (L, L), 1)
    hi = lax.Precision.HIGHEST
    acs = [jnp.dot((c <= r).astype(f32), dta[:, :nh], precision=hi, preferred_element_type=f32),
           jnp.dot((c >= r).astype(f32), dta[:, nh:], precision=hi, preferred_element_type=f32)]
    for d in range(2):
        act = acs[d].T
        for g in range(SSD_GROUPS):
            dt_ref[d, g] = dt[:, d * nh + g * hg:d * nh + (g + 1) * hg]
            ac_ref[d, g] = acs[d][:, g * hg:(g + 1) * hg]
            act_ref[d, g] = act[g * hg:(g + 1) * hg, :]


def ssd_prep(dt_raw, dt_bias, a_log):
    T, nh2 = dt_raw.shape
    nh = nh2 // 2
    hg = nh // SSD_GROUPS
    nc = T // SSD_CHUNK
    vec = pl.BlockSpec((1, nh2), lambda c: (0, 0))
    return pl.pallas_call(
        _ssd_prep_kernel,
        grid=(nc,),
        in_specs=[pl.BlockSpec((SSD_CHUNK, nh2), lambda c: (c, 0)), vec, vec],
        out_specs=[pl.BlockSpec((2, SSD_GROUPS, SSD_CHUNK, hg), lambda c: (0, 0, c, 0)),
                   pl.BlockSpec((2, SSD_GROUPS, SSD_CHUNK, hg), lambda c: (0, 0, c, 0)),
                   pl.BlockSpec((2, SSD_GROUPS, None, hg, SSD_CHUNK), lambda c: (0, 0, c, 0, 0))],
        out_shape=[jax.ShapeDtypeStruct((2, SSD_GROUPS, T, hg), f32),
                   jax.ShapeDtypeStruct((2, SSD_GROUPS, T, hg), f32),
                   jax.ShapeDtypeStruct((2, SSD_GROUPS, nc, hg, SSD_CHUNK), f32)],
        compiler_params=_cparams("parallel"),
        name="ssd_prep",
    )(dt_raw, dt_bias, a_log)


def _ssd_scan_kernel(*refs, reverse, final, hg):
    if final:
        (xs_ref, b_ref, c_ref, dt_ref, ac_ref, act_ref, s0_ref, yf_ref, z_ref, dsk_ref, ng_ref,
         out_ref, st_ref, yz_s) = refs
    else:
        (xs_ref, b_ref, c_ref, dt_ref, ac_ref, act_ref, s0_ref, out_ref, st_ref) = refs
    L = SSD_CHUNK
    P = SSD_HEADDIM
    ci = pl.program_id(0)
    g = pl.program_id(1)

    @pl.when(ci == 0)
    def _():
        st_ref[g] = s0_ref[g]

    xs = xs_ref[...]
    dt16 = dt_ref[...]
    ac16 = ac_ref[...]
    act = act_ref[...]
    expand = (lax.broadcasted_iota(i32, (hg, hg * P), 0)
              == lax.broadcasted_iota(i32, (hg, hg * P), 1) // P).astype(f32)
    dtx = jnp.dot(dt16, expand, precision=lax.Precision.HIGHEST, preferred_element_type=f32)
    xq = (xs * dtx).astype(bf16)
    bm = b_ref[...]
    cm = c_ref[...]
    cb = lax.dot_general(cm.astype(bf16), bm.astype(bf16), (((1,), (1,)), ((), ())), preferred_element_type=f32)
    bt = bm.T
    li = lax.broadcasted_iota(i32, (L, L), 0)
    si = lax.broadcasted_iota(i32, (L, L), 1)
    mask = (li <= si) if reverse else (li >= si)
    last = 0 if reverse else L - 1
    st = st_ref[g]
    stb = st.astype(bf16)
    ys = []
    new_st = []
    for hh in range(hg):
        hs = slice(hh * P, (hh + 1) * P)
        col = ac16[:, hh:hh + 1]
        row = act[hh:hh + 1, :]
        gm = (jnp.exp(jnp.where(mask, col - row, -jnp.inf)) * cb).astype(bf16)
        ce = (cm * jnp.exp(col)).astype(bf16)
        xqh = xq[:, hs]
        ys.append(jnp.dot(gm, xqh, preferred_element_type=f32)
                  + jnp.dot(ce, stb[:, hs], preferred_element_type=f32))
        tot = row[:, last:last + 1]
        bdt = (bt * jnp.exp(tot - row)).astype(bf16)
        new_st.append(st[:, hs] * jnp.exp(tot) + jnp.dot(bdt, xqh, preferred_element_type=f32))
    y = jnp.concatenate(ys, axis=1)
    st_ref[g] = jnp.concatenate(new_st, axis=1)
    if not final:
        out_ref[...] = y
        return
    yz_s[g] = (yf_ref[...] + y + dsk_ref[...] * xs) * jax.nn.silu(z_ref[...])

    @pl.when(g == SSD_GROUPS - 1)
    def _():
        w = hg * P
        ss = None
        for k in range(SSD_GROUPS):
            v = yz_s[k]
            s = jnp.sum(v * v, axis=-1, keepdims=True)
            ss = s if ss is None else ss + s
        rinv = lax.rsqrt(ss * (1.0 / (w * SSD_GROUPS)) + EPS)
        for k in range(SSD_GROUPS):
            out_ref[:, k * w:(k + 1) * w] = ((yz_s[k] * rinv) * ng_ref[:, k * w:(k + 1) * w]).astype(out_ref.dtype)


def ssd_scan(xbc, dtg, acg, act, s0, fin=None, *, reverse):
    T = xbc.shape[0]
    L = SSD_CHUNK
    nc = T // L
    hg = dtg.shape[-1]
    w = hg * SSD_HEADDIM
    d_inner = w * SSD_GROUPS
    nb0 = d_inner // SSD_STATE
    final = fin is not None
    cix = (lambda c: nc - 1 - c) if reverse else (lambda c: c)
    full_state = pl.BlockSpec(s0.shape, lambda c, g: (0, 0, 0))
    in_specs = [
        pl.BlockSpec((L, w), lambda c, g: (cix(c), g)),
        pl.BlockSpec((L, SSD_STATE), lambda c, g: (cix(c), nb0 + g)),
        pl.BlockSpec((L, SSD_STATE), lambda c, g: (cix(c), nb0 + SSD_GROUPS + g)),
        pl.BlockSpec((None, L, hg), lambda c, g: (g, cix(c), 0)),
        pl.BlockSpec((None, L, hg), lambda c, g: (g, cix(c), 0)),
        pl.BlockSpec((None, None, hg, L), lambda c, g: (g, cix(c), 0, 0)),
        full_state,
    ]
    args = [xbc, xbc, xbc, dtg, acg, act, s0]
    scratch = []
    if final:
        yf, zx, dsk, ng = fin
        in_specs += [pl.BlockSpec((L, w), lambda c, g: (cix(c), g)),
                     pl.BlockSpec((L, w), lambda c, g: (cix(c), g)),
                     pl.BlockSpec((1, w), lambda c, g: (0, g)),
                     pl.BlockSpec((1, d_inner), lambda c, g: (0, 0))]
        args += [yf, zx, dsk, ng]
        out_spec = pl.BlockSpec((L, d_inner), lambda c, g: (cix(c), 0))
        out_shape = jax.ShapeDtypeStruct((T, d_inner), bf16)
        scratch = [pltpu.VMEM((SSD_GROUPS, L, w), f32)]
    else:
        out_spec = pl.BlockSpec((L, w), lambda c, g: (cix(c), g))
        out_shape = jax.ShapeDtypeStruct((T, d_inner), f32)
    return pl.pallas_call(
        functools.partial(_ssd_scan_kernel, reverse=reverse, final=final, hg=hg),
        grid=(nc, SSD_GROUPS),
        in_specs=in_specs,
        out_specs=[out_spec, full_state],
        out_shape=[out_shape, jax.ShapeDtypeStruct(s0.shape, f32)],
        scratch_shapes=scratch,
        compiler_params=_cparams("arbitrary", "arbitrary"),
        name="ssd_scan_bwd" if reverse else "ssd_scan_fwd",
    )(*args)


def ssd_mixer(h_args_l, h_args_c, p):
    d_inner = p["norm_g"].shape[1]
    hg = d_inner // SSD_HEADDIM // SSD_GROUPS
    s_zero = jnp.zeros((SSD_GROUPS, SSD_STATE, hg * SSD_HEADDIM), f32)

    def prep(h_args):
        zx = norm_mod_matmul(*h_args, p["w_zx"])
        dt_raw = norm_mod_matmul(*h_args, p["w_dt"], tn=p["w_dt"].shape[1])
        xbc = conv_silu(zx, p["conv_w"], p["conv_b"], d_inner)
        return (zx, xbc) + tuple(ssd_prep(dt_raw, p["dt_bias"], p["a_log"]))

    zx_c, xbc_c, dt_c, ac_c, act_c = prep(h_args_c)
    zx_l, xbc_l, dt_l, ac_l, act_l = prep(h_args_l)
    yf_c, s_c = ssd_scan(xbc_c, dt_c[0], ac_c[0], act_c[0], s_zero, reverse=False)
    yf_l, _ = ssd_scan(xbc_l, dt_l[0], ac_l[0], act_l[0], s_c, reverse=False)
    y_c, s_c = ssd_scan(xbc_c, dt_c[1], ac_c[1], act_c[1], s_zero, (yf_c, zx_c, p["d_skip"], p["norm_g"]), reverse=True)
    y_l, _ = ssd_scan(xbc_l, dt_l[1], ac_l[1], act_l[1], s_c, (yf_l, zx_l, p["d_skip"], p["norm_g"]), reverse=True)
    return y_l, y_c


def _rope_table_kernel(cos_ref, sin_ref, *, tt):
    half = HEAD_DIM // 2
    t = lax.broadcasted_iota(i32, (tt, HEAD_DIM), 0) + pl.program_id(0) * tt
    lane = lax.broadcasted_iota(i32, (tt, HEAD_DIM), 1)
    pos = jnp.where(lane < half, t // GRID_W, t % GRID_W).astype(f32)
    k = (lane % (half // 2)).astype(f32)
    inv_freq = jnp.exp(k * (-2.0 / half * math.log(ROPE_BASE)))
    ang = pos * inv_freq
    cos_ref[...] = jnp.cos(ang)
    sin_ref[...] = jnp.where(lane % half < half // 2, -1.0, 1.0) * jnp.sin(ang)


def rope_tables(T, *, tt=512):
    tt = _row_tile(T, tt)
    spec = pl.BlockSpec((tt, HEAD_DIM), lambda i: (i, 0))
    return pl.pallas_call(
        functools.partial(_rope_table_kernel, tt=tt),
        grid=(T // tt,),
        in_specs=[],
        out_specs=[spec, spec],
        out_shape=[jax.ShapeDtypeStruct((T, HEAD_DIM), f32)] * 2,
        compiler_params=_cparams("parallel"),
        name="rope_tables",
    )()


def _rope(x, cos, sin):
    n, w = x.shape
    reps = w // HEAD_DIM
    if reps > 1:
        cos = jnp.concatenate([cos] * reps, axis=1)
        sin = jnp.concatenate([sin] * reps, axis=1)
    q = HEAD_DIM // 4
    lane = lax.broadcasted_iota(i32, (n, w), 1)
    partner = jnp.where(lane % (2 * q) < q, pltpu.roll(x, w - q, axis=1), pltpu.roll(x, q, axis=1))
    return x * cos + partner * sin


def _attn_kernel(*refs, band, nq, T):
    if band:
        (q_ref, kp_ref, ko_ref, kn_ref, vp_ref, vo_ref, vn_ref, cp_ref, co_ref, cn_ref, sp_ref, so_ref, sn_ref,
         kc_ref, vc_ref, sink_ref, o_ref) = refs
    else:
        q_ref, kc_ref, vc_ref, sink_ref, o_ref = refs
    nrow = q_ref.shape[0]
    q = q_ref[...]
    if band:
        q = _rope(q, co_ref[...], so_ref[...])
    qs = jnp.concatenate([q[:, j * HEAD_DIM:(j + 1) * HEAD_DIM] for j in range(ATTN_GROUP)], axis=0).astype(bf16)
    kc = kc_ref[...].astype(bf16)
    vc = vc_ref[...].astype(bf16)
    nt = (((1,), (1,)), ((), ()))
    scale = HEAD_DIM ** -0.5
    sink = sink_ref[...]
    s_c = lax.dot_general(qs, kc, nt, preferred_element_type=f32) * scale
    m = jnp.maximum(jnp.max(s_c, axis=1, keepdims=True), sink)
    if band:
        i = pl.program_id(1)
        kb = jnp.concatenate([_rope(kp_ref[...], cp_ref[...], sp_ref[...]),
                              _rope(ko_ref[...], co_ref[...], so_ref[...]),
                              _rope(kn_ref[...], cn_ref[...], sn_ref[...])], axis=0).astype(bf16)
        vb = jnp.concatenate([vp_ref[...], vo_ref[...], vn_ref[...]], axis=0).astype(bf16)
        s_b = lax.dot_general(qs, kb, nt, preferred_element_type=f32) * scale
        shp = s_b.shape
        qi = lax.broadcasted_iota(i32, shp, 0) % nrow + nrow
        ki = lax.broadcasted_iota(i32, shp, 1)
        k_abs = ki + (i - 1) * nrow
        valid = (jnp.abs(qi - ki) <= WINDOW) & (k_abs >= 0) & (k_abs < T)
        s_b = jnp.where(valid, s_b, -jnp.inf)
        m = jnp.maximum(m, jnp.max(s_b, axis=1, keepdims=True))
        p_b = jnp.exp(s_b - m)
    p_c = jnp.exp(s_c - m)
    den = jnp.sum(p_c, axis=1, keepdims=True) + jnp.exp(sink - m)
    if band:
        den = den + jnp.sum(p_b, axis=1, keepdims=True)
    rden = 1.0 / den
    o = jnp.dot((p_c * rden).astype(bf16), vc, preferred_element_type=f32)
    if band:
        o = o + jnp.dot((p_b * rden).astype(bf16), vb, preferred_element_type=f32)
    o_ref[...] = jnp.concatenate([o[j * nrow:(j + 1) * nrow, :] for j in range(ATTN_GROUP)], axis=1).astype(o_ref.dtype)


def window_attention(qkv_l, qkv_c, sink_rows, cos, sin):
    T = qkv_l.shape[0]
    Lc = qkv_c.shape[0]
    nkv = qkv_l.shape[1] // HEAD_DIM // (ATTN_GROUP + 2)
    nh = nkv * ATTN_GROUP
    k0, v0 = nh, nh + nkv
    gw = ATTN_GROUP * HEAD_DIM
    blk = WINDOW
    nq = T // blk
    prv = lambda i: jnp.maximum(i - 1, 0)
    nxt = lambda i: jnp.minimum(i + 1, nq - 1)
    rows = [prv, lambda i: i, nxt]
    kv = lambda c0: [pl.BlockSpec((blk, HEAD_DIM), (lambda g, i, f=f, c0=c0: (f(i), c0 + g))) for f in rows]
    tab = [pl.BlockSpec((blk, HEAD_DIM), (lambda g, i, f=f: (f(i), 0))) for f in rows]
    ctx_specs = lambda: [pl.BlockSpec((Lc, HEAD_DIM), lambda g, i: (0, k0 + g)),
                         pl.BlockSpec((Lc, HEAD_DIM), lambda g, i: (0, v0 + g))]
    o_l = pl.pallas_call(
        functools.partial(_attn_kernel, band=True, nq=nq, T=T),
        grid=(nkv, nq),
        in_specs=[pl.BlockSpec((blk, gw), lambda g, i: (i, g))] + kv(k0) + kv(v0) + tab + tab + ctx_specs()
        + [pl.BlockSpec((None, ATTN_GROUP * blk, 1), lambda g, i: (g, 0, 0))],
        out_specs=pl.BlockSpec((blk, gw), lambda g, i: (i, g)),
        out_shape=jax.ShapeDtypeStruct((T, nh * HEAD_DIM), bf16),
        compiler_params=_cparams("parallel", "parallel"),
        name="window_attention",
    )(qkv_l, *([qkv_l] * 6), cos, cos, cos, sin, sin, sin, qkv_c, qkv_c, sink_rows[0])
    o_c = pl.pallas_call(
        functools.partial(_attn_kernel, band=False, nq=1, T=Lc),
        grid=(nkv, 1),
        in_specs=[pl.BlockSpec((Lc, gw), lambda g, i: (0, g))] + ctx_specs()
        + [pl.BlockSpec((None, ATTN_GROUP * Lc, 1), lambda g, i: (g, 0, 0))],
        out_specs=pl.BlockSpec((Lc, gw), lambda g, i: (0, g)),
        out_shape=jax.ShapeDtypeStruct((Lc, nh * HEAD_DIM), bf16),
        compiler_params=_cparams("parallel", "parallel"),
        name="context_attention",
    )(qkv_c, qkv_c, qkv_c, sink_rows[1])
    return o_l, o_c


def _adaln_kernel(c_ref, dn_ref, up_ref, b_ref, o_ref):
    hi = lax.Precision.HIGHEST
    t = jnp.dot(jax.nn.silu(c_ref[...]), dn_ref[...], precision=hi, preferred_element_type=f32)
    o_ref[...] = jnp.dot(t, up_ref[...], precision=hi, preferred_element_type=f32) + b_ref[...]


def adaln(cond, w_down, w_up, b, *, tn=2048):
    depth, d, r = w_down.shape
    n = w_up.shape[2]
    rows = cond.shape[0]
    return pl.pallas_call(
        _adaln_kernel,
        grid=(depth, n // tn),
        in_specs=[pl.BlockSpec((rows, d), lambda l, j: (0, 0)),
                  pl.BlockSpec((None, d, r), lambda l, j: (l, 0, 0)),
                  pl.BlockSpec((None, r, tn), lambda l, j: (l, 0, j)),
                  pl.BlockSpec((None, 1, tn), lambda l, j: (l, 0, j))],
        out_specs=pl.BlockSpec((None, rows, tn), lambda l, j: (l, 0, j)),
        out_shape=jax.ShapeDtypeStruct((depth, rows, n), f32),
        compiler_params=_cparams("parallel", "parallel"),
        name="adaln",
    )(cond, w_down, w_up, b.reshape(depth, 1, n))


def kernel(x, c, ctx, c_ctx, ada_down, ada_up, ada_b, norm_g, router_w, moe_w_gate, moe_w_up, moe_w_down,
           rg_w_in, rg_conv_w, rg_conv_b, rg_gate_w, rg_gate_b, rg_lambda, rg_w_out,
           ssd_w_in, ssd_conv_w, ssd_conv_b, ssd_dt_bias, ssd_a_log, ssd_d, ssd_norm_g, ssd_w_out,
           attn_w_qkv, attn_sink, attn_w_out):
    assert x.shape[0] == 1 and ctx.shape[0] == 1 and c.shape[0] == 1
    depth = ada_down.shape[0]
    T, d = x.shape[1], x.shape[2]
    Lc = ctx.shape[1]
    xl, xc = x[0], ctx[0]
    cond = jnp.concatenate([c, c_ctx[None, :], jnp.zeros((SUBLANES - 2, d), f32)], axis=0)
    mods = adaln(cond, ada_down, ada_up, ada_b)
    row = lambda v: v.reshape(1, -1)

    for i in range(depth):
        kind, slot = i % N_MIXERS, i // N_MIXERS
        ml = [mods[i, 0:1, k * d:(k + 1) * d] for k in range(N_MOD)]
        mc = [mods[i, 1:2, k * d:(k + 1) * d] for k in range(N_MOD)]
        g = [row(norm_g[i, k]) for k in range(4)]
        last = i == depth - 1
        in_l = (xl, g[0], ml[0], ml[1])
        in_c = (xc, g[0], mc[0], mc[1])
        if kind == 0:
            p = dict(w_in=rg_w_in[slot].astype(bf16), conv_w=rg_conv_w[slot], conv_b=row(rg_conv_b[slot]),
                     gate_w=rg_gate_w[slot].astype(bf16), gate_b=rg_gate_b[slot],
                     lam=rg_lambda[slot].reshape(2, 1, -1))
            y_l, y_c = rglru_mixer(in_l, in_c, p)
            w_out = rg_w_out[slot].astype(bf16)
        elif kind == 1:
            d_inner = ssd_norm_g.shape[1]
            w_in = ssd_w_in[slot]
            n_zx = d_inner + ssd_conv_w.shape[2]
            p = dict(w_zx=w_in[:, :n_zx].astype(bf16), w_dt=w_in[:, n_zx:].astype(bf16),
                     conv_w=ssd_conv_w[slot], conv_b=row(ssd_conv_b[slot]),
                     dt_bias=row(ssd_dt_bias[slot]), a_log=row(ssd_a_log[slot]),
                     d_skip=row(jnp.repeat(ssd_d[slot], SSD_HEADDIM)), norm_g=row(ssd_norm_g[slot]))
            y_l, y_c = ssd_mixer(in_l, in_c, p)
            w_out = ssd_w_out[slot].astype(bf16)
        else:
            w_qkv = attn_w_qkv[slot].astype(bf16)
            qkv_l = norm_mod_matmul(*in_l, w_qkv)
            qkv_c = norm_mod_matmul(*in_c, w_qkv)
            sk = attn_sink[slot].reshape(-1, ATTN_GROUP, 1, 1)
            sink_rows = [jnp.broadcast_to(sk, sk.shape[:2] + (n, 1)).reshape(sk.shape[0], ATTN_GROUP * n, 1)
                         for n in (WINDOW, Lc)]
            cos, sin = rope_tables(T)
            y_l, y_c = window_attention(qkv_l, qkv_c, sink_rows, cos, sin)
            w_out = attn_w_out[slot].astype(bf16)
        mp = dict(router_wt=router_w[i].T, w_gate=moe_w_gate[i].astype(bf16), w_up=moe_w_up[i].astype(bf16),
                  w_down=moe_w_down[i].astype(bf16))
        xl = matmul_norm_res(y_l, w_out, g[1], ml[2], xl)
        xl = moe_block(xl, g[2], ml[3], ml[4], g[3], ml[5], mp)
        if not last:
            xc = matmul_norm_res(y_c, w_out, g[1], mc[2], xc)
            xc = moe_block(xc, g[2], mc[3], mc[4], g[3], mc[5], mp)
    return xl[None]
```

```python
import functools
import math

import jax
import jax.numpy as jnp
from jax import lax
from jax.experimental import pallas as pl
from jax.experimental.pallas import tpu as pltpu

f32 = jnp.float32
bf16 = jnp.bfloat16
i32 = jnp.int32

EPS = 1e-6
N_MIXERS = 3
N_MOD = 6
CONV_W = 4
RG_HEADS = 16
RG_C = 8.0
SSD_HEADDIM = 64
SSD_STATE = 128
SSD_GROUPS = 8
SSD_CHUNK = 128
HEAD_DIM = 128
ATTN_GROUP = 4
WINDOW = 128
ROPE_BASE = 10000.0
GRID_W = 64
N_EXPERTS = 16
EC_CAPACITY = 2

LANES = 128
SUBLANES = 8
VMEM_LIMIT = 56 * 1024 * 1024


def _cparams(*sem):
    return pltpu.CompilerParams(dimension_semantics=sem, vmem_limit_bytes=VMEM_LIMIT)


def _row_tile(m, want):
    t = min(m, want)
    assert m % t == 0
    return t


def _sigmoid(x):
    return 0.5 * jnp.tanh(0.5 * x) + 0.5


def _silu(x):
    return x * _sigmoid(x)


def _rms_mod(x, g, shift, scale):
    y = x * lax.rsqrt(jnp.mean(x * x, axis=-1, keepdims=True) + EPS)
    return (y * g) * (1.0 + scale) + shift


def _nmm_kernel(x_ref, g_ref, sh_ref, sc_ref, w_ref, o_ref, a_ref):
    @pl.when(pl.program_id(1) == 0)
    def _():
        a_ref[...] = _rms_mod(x_ref[...], g_ref[...], sh_ref[...], sc_ref[...]).astype(a_ref.dtype)

    o_ref[...] = jnp.dot(a_ref[...], w_ref[...], preferred_element_type=f32).astype(o_ref.dtype)


def norm_mod_matmul(x, g, shift, scale, w, *, tm=512, tn=1024, out_dtype=f32):
    m, d = x.shape
    n = w.shape[1]
    tm = _row_tile(m, tm)
    tn = _row_tile(n, tn)
    vec = pl.BlockSpec((1, d), lambda i, j: (0, 0))
    return pl.pallas_call(
        _nmm_kernel,
        grid=(m // tm, n // tn),
        in_specs=[pl.BlockSpec((tm, d), lambda i, j: (i, 0)), vec, vec, vec,
                  pl.BlockSpec((d, tn), lambda i, j: (0, j))],
        out_specs=pl.BlockSpec((tm, tn), lambda i, j: (i, j)),
        out_shape=jax.ShapeDtypeStruct((m, n), out_dtype),
        scratch_shapes=[pltpu.VMEM((tm, d), bf16)],
        compiler_params=_cparams("parallel", "arbitrary"),
        name="norm_mod_matmul",
    )(x, g, shift, scale, w)


def _mnr_kernel(a_ref, w_ref, g_ref, gate_ref, r_ref, o_ref, y_ref, ss_ref, *, nj, n_total):
    j = pl.program_id(1)

    @pl.when(j == 0)
    def _():
        ss_ref[...] = jnp.zeros_like(ss_ref)

    @pl.when(j < nj)
    def _():
        y = jnp.dot(a_ref[...], w_ref[...], preferred_element_type=f32)
        y_ref[j] = y
        ss_ref[...] += jnp.sum(y * y, axis=-1, keepdims=True)

    @pl.when(j >= nj)
    def _():
        rinv = lax.rsqrt(ss_ref[...] * (1.0 / n_total) + EPS)
        o_ref[...] = r_ref[...] + gate_ref[...] * ((y_ref[j - nj] * rinv) * g_ref[...])


def matmul_norm_res(a, w, g, gate, resid, *, tn=512):
    m, k = a.shape
    n = w.shape[1]
    tm = _row_tile(m, 4 * 1024 * 1024 // k)
    tn = _row_tile(n, tn)
    nj = n // tn
    ph2 = lambda j: jnp.maximum(j - nj, 0)
    return pl.pallas_call(
        functools.partial(_mnr_kernel, nj=nj, n_total=n),
        grid=(m // tm, 2 * nj),
        in_specs=[pl.BlockSpec((tm, k), lambda i, j: (i, 0)),
                  pl.BlockSpec((k, tn), lambda i, j: (0, jnp.minimum(j, nj - 1))),
                  pl.BlockSpec((1, tn), lambda i, j: (0, ph2(j))),
                  pl.BlockSpec((1, tn), lambda i, j: (0, ph2(j))),
                  pl.BlockSpec((tm, tn), lambda i, j: (i, ph2(j)))],
        out_specs=pl.BlockSpec((tm, tn), lambda i, j: (i, ph2(j))),
        out_shape=jax.ShapeDtypeStruct((m, n), f32),
        scratch_shapes=[pltpu.VMEM((nj, tm, tn), f32), pltpu.VMEM((tm, 1), f32)],
        compiler_params=_cparams("parallel", "arbitrary"),
        name="matmul_norm_res",
    )(a, w, g, gate, resid)


def _softplus(x):
    return jnp.maximum(x, 0.0) + jnp.log1p(jnp.exp(-jnp.abs(x)))


def _rg_sweep_kernel(*refs, reverse, final, nt, tt):
    if final:
        (v_ref, vp_ref, vn_ref, cw_ref, cb_ref, gw_ref, gb_ref, lam_ref, h0_ref, hf_ref, g_ref,
         out_ref, hT_ref, ext_s, a_s, u_s, hs_s, carry_s) = refs
    else:
        (v_ref, vp_ref, vn_ref, cw_ref, cb_ref, gw_ref, gb_ref, lam_ref, h0_ref,
         out_ref, hT_ref, ext_s, a_s, u_s, hs_s, carry_s) = refs
    t = pl.program_id(1)
    ti = nt - 1 - t if reverse else t
    c = v_ref.shape[1]
    s_len = tt // SUBLANES

    @pl.when(t == 0)
    def _():
        carry_s[...] = h0_ref[...]

    ext_s[0:8, :] = jnp.where(ti == 0, 0.0, vp_ref[...])
    ext_s[8:8 + tt, :] = v_ref[...]
    ext_s[8 + tt:16 + tt, :] = jnp.where(ti == nt - 1, 0.0, vn_ref[...])
    vc = cb_ref[...]
    for k in range(CONV_W):
        vc = vc + ext_s[pl.ds(6 + k, tt), :] * cw_ref[k:k + 1, :]

    vb = vc.astype(bf16)
    hw = gw_ref.shape[-1]
    heads = [vb[:, k * hw:(k + 1) * hw] for k in range(c // hw)]
    gate = lambda n: jnp.concatenate([jnp.dot(vh, gw_ref[n, k], preferred_element_type=f32)
                                      for k, vh in enumerate(heads)], axis=1)
    gr = gate(0) + gb_ref[0:1, :]
    gi = gate(1) + gb_ref[1:2, :]
    r = _sigmoid(gr)
    ig = _sigmoid(gi)
    log_a = (-RG_C) * r * _softplus(-lam_ref[...])
    a = jnp.exp(log_a)
    u = jnp.sqrt(-jnp.tanh(log_a) * (a * a + 1.0)) * (ig * vc)

    a_s[...] = a.reshape(SUBLANES, s_len, c)
    u_s[...] = u.reshape(SUBLANES, s_len, c)

    def jj(j):
        return s_len - 1 - j if reverse else j

    def pass1(j, hp):
        h, p = hp
        av = a_s[:, jj(j), :]
        return av * h + u_s[:, jj(j), :], av * p

    hfin, pfin = lax.fori_loop(0, s_len, pass1, (jnp.zeros((SUBLANES, c), f32), jnp.ones((SUBLANES, c), f32)),
                               unroll=8)
    cur = carry_s[...]
    cins = [None] * SUBLANES
    for s in (range(SUBLANES - 1, -1, -1) if reverse else range(SUBLANES)):
        cins[s] = cur
        cur = hfin[s:s + 1, :] + pfin[s:s + 1, :] * cur
    carry_s[...] = cur
    hT_ref[...] = cur

    def pass2(j, h):
        h = a_s[:, jj(j), :] * h + u_s[:, jj(j), :]
        hs_s[:, jj(j), :] = h
        return h

    lax.fori_loop(0, s_len, pass2, jnp.concatenate(cins, axis=0), unroll=8)
    hseq = hs_s[...].reshape(tt, c)
    if final:
        out_ref[...] = ((hf_ref[...] + hseq) * jax.nn.gelu(g_ref[...])).astype(out_ref.dtype)
    else:
        out_ref[...] = hseq


def rg_sweep(gv, conv_w, conv_b, gate_w, gate_b, lam, h0, hf=None, *, reverse, tt=512, heads_per_block=4):
    T, d2 = gv.shape
    d = d2 // 2
    hw = d // RG_HEADS
    c = heads_per_block * hw
    nh = d // c
    tt = _row_tile(T, tt)
    nt = T // tt
    final = hf is not None
    tb = tt // 8
    tix = (lambda t: nt - 1 - t) if reverse else (lambda t: t)
    head_vec = lambda rows: pl.BlockSpec((rows, c), lambda h, t: (0, h))
    in_specs = [
        pl.BlockSpec((tt, c), lambda h, t: (tix(t), nh + h)),
        pl.BlockSpec((8, c), lambda h, t: (jnp.maximum(tix(t) * tb - 1, 0), nh + h)),
        pl.BlockSpec((8, c), lambda h, t: (jnp.minimum((tix(t) + 1) * tb, T // 8 - 1), nh + h)),
        head_vec(CONV_W), head_vec(1),
        pl.BlockSpec((2, heads_per_block, hw, hw), lambda h, t: (0, h, 0, 0)),
        head_vec(2), head_vec(1), head_vec(1),
    ]
    args = [gv, gv, gv, conv_w, conv_b, gate_w, gate_b, lam, h0]
    if final:
        in_specs += [pl.BlockSpec((tt, c), lambda h, t: (tix(t), h)),
                     pl.BlockSpec((tt, c), lambda h, t: (tix(t), h))]
        args += [hf, gv]
    return pl.pallas_call(
        functools.partial(_rg_sweep_kernel, reverse=reverse, final=final, nt=nt, tt=tt),
        grid=(nh, nt),
        in_specs=in_specs,
        out_specs=[pl.BlockSpec((tt, c), lambda h, t: (tix(t), h)), head_vec(1)],
        out_shape=[jax.ShapeDtypeStruct((T, d), bf16 if final else f32), jax.ShapeDtypeStruct((1, d), f32)],
        scratch_shapes=[pltpu.VMEM((tt + 16, c), f32)] + [pltpu.VMEM((SUBLANES, tt // SUBLANES, c), f32)] * 3
        + [pltpu.VMEM((1, c), f32)],
        compiler_params=_cparams("parallel", "arbitrary"),
        name="rg_sweep_bwd" if reverse else "rg_sweep_fwd",
    )(*args)


def rglru_mixer(h_args_l, h_args_c, p):
    w_in = p["w_in"]
    zeros = jnp.zeros((1, w_in.shape[0]), f32)
    gv_c = norm_mod_matmul(*h_args_c, w_in)
    gv_l = norm_mod_matmul(*h_args_l, w_in)
    sw = lambda gv, d, h0, hf, rev: rg_sweep(gv, p["conv_w"], p["conv_b"], p["gate_w"][d], p["gate_b"][d],
                                             p["lam"][d], h0, hf, reverse=rev)
    hf_c, s_c = sw(gv_c, 0, zeros, None, False)
    hf_l, _ = sw(gv_l, 0, s_c, None, False)
    y_c, s_c = sw(gv_c, 1, zeros, hf_c, True)
    y_l, _ = sw(gv_l, 1, s_c, hf_l, True)
    return y_l, y_c


def _slab(first, count, nq):
    start = first * nq
    if not isinstance(start, int):
        start = pl.multiple_of(start, nq)
    return pl.ds(start, count * nq)


def _router_kernel(x_ref, g_ref, sh_ref, sc_ref, rw_ref, h_ref, aff_ref):
    h = _rms_mod(x_ref[...], g_ref[...], sh_ref[...], sc_ref[...])
    tm, d = h.shape
    nq = d // LANES
    for q in range(nq):
        h_ref[pl.ds(q, tm, stride=nq), :] = h[:, q * LANES:(q + 1) * LANES]
    logits = lax.dot_general(rw_ref[...], h, (((1,), (1,)), ((), ())), precision=lax.Precision.HIGHEST,
                             preferred_element_type=f32)
    e = jnp.exp(logits - jnp.max(logits, axis=0, keepdims=True))
    aff_ref[...] = e / jnp.sum(e, axis=0, keepdims=True)


def moe_router(x, g, shift, scale, router_wt, *, tm=512):
    m, d = x.shape
    ne = router_wt.shape[0]
    tm = _row_tile(m, tm)
    vec = pl.BlockSpec((1, d), lambda i: (0, 0))
    return pl.pallas_call(
        _router_kernel,
        grid=(m // tm,),
        in_specs=[pl.BlockSpec((tm, d), lambda i: (i, 0)), vec, vec, vec,
                  pl.BlockSpec((ne, d), lambda i: (0, 0))],
        out_specs=[pl.BlockSpec((tm * (d // LANES), LANES), lambda i: (i, 0)),
                   pl.BlockSpec((ne, tm), lambda i: (0, i))],
        out_shape=[jax.ShapeDtypeStruct((m * (d // LANES), LANES), f32),
                   jax.ShapeDtypeStruct((ne, m), f32)],
        compiler_params=_cparams("parallel"),
        name="moe_router",
    )(x, g, shift, scale, router_wt)


def _select_kernel(aff_ref, idx_ref, gate_ref, off_ref, num_ref, cnt_s, *, cap, jb):
    nb = aff_ref.shape[0]
    hi = lax.Precision.HIGHEST
    aff = aff_ref[...]
    keys = pltpu.bitcast(aff, i32)

    def total(m):
        return jnp.sum(jnp.sum(m.astype(i32), axis=0, keepdims=True), axis=1, keepdims=True)

    def bs(i, thr):
        cand = thr | (jnp.int32(1) << (30 - i))
        return jnp.where(total(keys >= cand) >= cap, cand, thr)

    thr = lax.fori_loop(0, 31, bs, jnp.zeros((1, 1), i32))
    gt = keys > thr
    eq = keys == thr
    need = (cap - total(gt)).astype(f32)

    tri = (lax.broadcasted_iota(i32, (LANES, LANES), 0) <= lax.broadcasted_iota(i32, (LANES, LANES), 1)).astype(bf16)
    bi = lax.broadcasted_iota(i32, (nb, nb), 0)
    bj = lax.broadcasted_iota(i32, (nb, nb), 1)

    def block_counts(m):
        within = jnp.dot(m.astype(f32).astype(bf16), tri, preferred_element_type=f32)
        rows = jnp.broadcast_to(within[:, LANES - 1:LANES], (nb, LANES))
        before = jnp.dot((bj < bi).astype(f32), rows, precision=hi, preferred_element_type=f32)
        return within, rows, before

    w_eq, _, b_eq = block_counts(eq)
    sel = gt | (eq & (b_eq + w_eq - eq.astype(f32) < need))
    w_sel, rows, before = block_counts(sel)
    cnt_s[...] = jnp.where(sel, before + w_sel, 0.0)
    off_ref[...] = before[:, 0:1].astype(i32)
    num_ref[...] = rows[:, 0:1].astype(i32)

    sel_b = sel.astype(f32).astype(bf16)
    per_block = lax.dot_general(jnp.ones((SUBLANES, LANES), bf16), sel_b, (((1,), (1,)), ((), ())),
                                preferred_element_type=f32)
    cum_end = jnp.dot(per_block, (bi <= bj).astype(f32), precision=hi, preferred_element_type=f32)[0:1, :]
    lane_b = lax.broadcasted_iota(i32, (jb, nb), 1).astype(f32)
    lane_t = lax.broadcasted_iota(i32, (jb, LANES), 1).astype(f32)
    row = lax.broadcasted_iota(i32, (jb, 1), 0).astype(f32)

    def group(r, carry):
        j0 = r * jb
        slot = row + lax.convert_element_type(j0, f32)
        blk = jnp.sum((cum_end <= slot).astype(f32), axis=1, keepdims=True)
        onehot = (lane_b == blk).astype(f32)
        cnt_rows = jnp.dot(onehot, cnt_s[...], precision=hi, preferred_element_type=f32)
        aff_rows = jnp.dot(onehot, aff_ref[...], precision=hi, preferred_element_type=f32)
        hit = cnt_rows == slot + 1.0
        tok = blk * LANES + jnp.sum(jnp.where(hit, lane_t, 0.0), axis=1, keepdims=True)
        idx_ref[pl.ds(j0, jb), :] = tok.astype(i32)
        gate_ref[pl.ds(j0, jb), :] = jnp.sum(jnp.where(hit, aff_rows, 0.0), axis=1, keepdims=True)
        return carry

    lax.fori_loop(0, cap // jb, group, 0)


def moe_select(aff_t, cap):
    ne, T = aff_t.shape
    jb = min(cap, LANES)
    tile = LANES * LANES
    tp = -(-T // tile) * tile
    aff3 = jnp.pad(aff_t, ((0, 0), (0, tp - T))).reshape(ne, tp // LANES, LANES)
    nb = tp // LANES
    slots = pl.BlockSpec((None, cap, 1), lambda e: (e, 0, 0))
    blocks = pl.BlockSpec((None, nb, 1), lambda e: (e, 0, 0))
    return pl.pallas_call(
        functools.partial(_select_kernel, cap=cap, jb=jb),
        grid=(ne,),
        in_specs=[pl.BlockSpec((None, nb, LANES), lambda e: (e, 0, 0))],
        out_specs=[slots, slots, blocks, blocks],
        out_shape=[jax.ShapeDtypeStruct((ne, cap, 1), i32), jax.ShapeDtypeStruct((ne, cap, 1), f32),
                   jax.ShapeDtypeStruct((ne, nb, 1), i32), jax.ShapeDtypeStruct((ne, nb, 1), i32)],
        scratch_shapes=[pltpu.VMEM((nb, LANES), f32)],
        compiler_params=_cparams("parallel"),
        name="moe_select",
    )(aff3)


def _expert_kernel(idx_ref, gate_ref, h_hbm, wgu_ref, wd_ref, ye_ref, xbuf, sem, *, rows, nsteps):
    step = pl.program_id(0) * pl.num_programs(1) + pl.program_id(1)
    slot = step % 2
    d = wgu_ref.shape[0]
    nq = d // LANES

    def gather(stp, slt):
        def body(r, c):
            t = idx_ref[stp * rows + r]
            pltpu.make_async_copy(h_hbm.at[_slab(t, 1, nq)], xbuf.at[slt, _slab(r, 1, nq)], sem.at[slt]).start()
            return c

        lax.fori_loop(0, rows, body, 0, unroll=8)

    @pl.when(step == 0)
    def _():
        gather(step, slot)

    @pl.when(step + 1 < nsteps)
    def _():
        gather(step + 1, 1 - slot)

    pltpu.make_async_copy(h_hbm.at[pl.ds(0, rows * nq)], xbuf.at[slot], sem.at[slot]).wait()
    x = jnp.concatenate([xbuf[slot, pl.ds(q, rows, stride=nq), :].astype(bf16) for q in range(nq)], axis=1)
    gu = jnp.dot(x, wgu_ref[...], preferred_element_type=f32)
    ff = gu.shape[1] // 2
    hid = (_silu(gu[:, :ff]) * gu[:, ff:]).astype(bf16)
    gate = gate_ref[...]
    nw = 4 * LANES
    for c in range(d // nw):
        y = jnp.dot(hid, wd_ref[:, c * nw:(c + 1) * nw], preferred_element_type=f32) * gate
        for i in range(nw // LANES):
            ye_ref[pl.ds(c * (nw // LANES) + i, rows, stride=nq), :] = y[:, i * LANES:(i + 1) * LANES]


def moe_experts(idx_flat, gate, h2, w_gu, w_down, *, rows=256):
    ne, cap, _ = gate.shape
    d, ff = w_down.shape[2], w_down.shape[1]
    nq = d // LANES
    rows = _row_tile(cap, rows)
    nblk = cap // rows
    return pl.pallas_call(
        functools.partial(_expert_kernel, rows=rows, nsteps=ne * nblk),
        grid_spec=pltpu.PrefetchScalarGridSpec(
            num_scalar_prefetch=1,
            grid=(ne, nblk),
            in_specs=[pl.BlockSpec((None, rows, 1), lambda e, j, idx: (e, j, 0)),
                      pl.BlockSpec(memory_space=pl.ANY),
                      pl.BlockSpec((None, d, 2 * ff), lambda e, j, idx: (e, 0, 0)),
                      pl.BlockSpec((None, ff, d), lambda e, j, idx: (e, 0, 0))],
            out_specs=pl.BlockSpec((rows * nq, LANES), lambda e, j, idx: (e * nblk + j, 0)),
            scratch_shapes=[pltpu.VMEM((2, rows * nq, LANES), f32), pltpu.SemaphoreType.DMA((2,))]),
        out_shape=jax.ShapeDtypeStruct((ne * cap * nq, LANES), f32),
        compiler_params=_cparams("arbitrary", "arbitrary"),
        name="moe_experts",
    )(idx_flat, gate, h2, w_gu, w_down)


COMBINE_CHUNK_LOG2 = 4
COMBINE_CHUNK = 1 << COMBINE_CHUNK_LOG2


def _combine_kernel(idx_ref, off_ref, num_ref, ye_hbm, x_ref, g_ref, gate_ref, o_ref, stage, acc, sem,
                    *, cap, ne, nbk):
    b = pl.program_id(0)
    tb, d = x_ref.shape
    nq = d // LANES
    ch = COMBINE_CHUNK

    def chunks(bb, e, slt, wait):
        off = off_ref[e * nbk + bb]
        n = num_ref[e * nbk + bb]

        def body(c, carry):
            s0 = jnp.minimum(off + c * ch, cap - ch)
            cp = pltpu.make_async_copy(ye_hbm.at[_slab(e * cap + s0, ch, nq)], stage.at[slt, _slab(c * ch, ch, nq)],
                                       sem.at[slt])
            if wait:
                cp.wait()
            else:
                cp.start()
            return carry

        lax.fori_loop(0, lax.shift_right_logical(n + ch - 1, COMBINE_CHUNK_LOG2), body, 0)

    @pl.when(b == 0)
    def _():
        chunks(b, 0, 0, False)

    acc[...] = jnp.zeros_like(acc)
    for e in range(ne):
        slt = e % 2
        if e + 1 < ne:
            chunks(b, e + 1, 1 - slt, False)
        else:
            @pl.when(b + 1 < nbk)
            def _():
                chunks(b + 1, 0, 1 - slt, False)
        chunks(b, e, slt, True)
        off = off_ref[e * nbk + b]
        n = num_ref[e * nbk + b]

        def add_row(j, carry, e=e, slt=slt, off=off):
            c = lax.shift_right_logical(j - off, COMBINE_CHUNK_LOG2)
            s0 = jnp.minimum(off + c * ch, cap - ch)
            t = idx_ref[e * cap + j] - b * tb
            acc[_slab(t, 1, nq), :] = acc[_slab(t, 1, nq), :] + stage[slt, _slab(c * ch + (j - s0), 1, nq), :]
            return carry

        lax.fori_loop(off, off + n, add_row, 0)

    sq = jnp.zeros((tb, LANES), f32)
    for q in range(nq):
        part = acc[pl.ds(q, tb, stride=nq), :]
        sq = sq + part * part
    rinv = lax.rsqrt(jnp.sum(sq, axis=-1, keepdims=True) * (1.0 / d) + EPS)
    for q in range(nq):
        sl = slice(q * LANES, (q + 1) * LANES)
        part = acc[pl.ds(q, tb, stride=nq), :]
        o_ref[:, sl] = x_ref[:, sl] + gate_ref[:, sl] * ((part * rinv) * g_ref[:, sl])


def moe_combine(idx_flat, off, num, ye, x, g, gate, *, cap):
    T, d = x.shape
    nq = d // LANES
    tb = LANES
    nbk = T // tb
    ne = off.shape[0] // nbk
    assert ne % 2 == 0 and cap >= COMBINE_CHUNK
    vec = pl.BlockSpec((1, d), lambda i, *_: (0, 0))
    return pl.pallas_call(
        functools.partial(_combine_kernel, cap=cap, ne=ne, nbk=nbk),
        grid_spec=pltpu.PrefetchScalarGridSpec(
            num_scalar_prefetch=3,
            grid=(nbk,),
            in_specs=[pl.BlockSpec(memory_space=pl.ANY),
                      pl.BlockSpec((tb, d), lambda i, *_: (i, 0)), vec, vec],
            out_specs=pl.BlockSpec((tb, d), lambda i, *_: (i, 0)),
            scratch_shapes=[pltpu.VMEM((2, tb * nq, LANES), f32), pltpu.VMEM((tb * nq, LANES), f32),
                            pltpu.SemaphoreType.DMA((2,))]),
        out_shape=jax.ShapeDtypeStruct((T, d), f32),
        compiler_params=_cparams("arbitrary"),
        name="moe_combine",
    )(idx_flat, off, num, ye, x, g, gate)


def moe_block(x, g_in, shift, scale, g_out, gate, p):
    T = x.shape[0]
    cap = EC_CAPACITY * T // N_EXPERTS
    nbk = T // LANES
    h3, aff_t = moe_router(x, g_in, shift, scale, p["router_wt"])
    idx, gates, off, num = moe_select(aff_t, cap)
    idx_flat = idx.reshape(-1)
    ye = moe_experts(idx_flat, gates, h3, p["w_gu"], p["w_down"], rows=min(cap, 256))
    return moe_combine(idx_flat, off[:, :nbk, 0].reshape(-1), num[:, :nbk, 0].reshape(-1), ye, x, g_out, gate, cap=cap)


def _conv_silu_kernel(x_ref, xp_ref, xn_ref, cw_ref, cb_ref, o_ref, ext_s, *, nt, tt):
    ti = pl.program_id(0)
    ext_s[0:8, :] = jnp.where(ti == 0, 0.0, xp_ref[...])
    ext_s[8:8 + tt, :] = x_ref[...]
    ext_s[8 + tt:16 + tt, :] = jnp.where(ti == nt - 1, 0.0, xn_ref[...])
    acc = cb_ref[...]
    for k in range(CONV_W):
        acc = acc + ext_s[pl.ds(6 + k, tt), :] * cw_ref[k:k + 1, :]
    o_ref[...] = _silu(acc)


def conv_silu(zx, conv_w, conv_b, col0, *, tt=512, tc=1024):
    T = zx.shape[0]
    n = conv_w.shape[1]
    tt = _row_tile(T, tt)
    nt = T // tt
    tb = tt // 8
    cb0 = col0 // tc
    return pl.pallas_call(
        functools.partial(_conv_silu_kernel, nt=nt, tt=tt),
        grid=(nt, n // tc),
        in_specs=[pl.BlockSpec((tt, tc), lambda t, j: (t, cb0 + j)),
                  pl.BlockSpec((8, tc), lambda t, j: (jnp.maximum(t * tb - 1, 0), cb0 + j)),
                  pl.BlockSpec((8, tc), lambda t, j: (jnp.minimum((t + 1) * tb, T // 8 - 1), cb0 + j)),
                  pl.BlockSpec((CONV_W, tc), lambda t, j: (0, j)),
                  pl.BlockSpec((1, tc), lambda t, j: (0, j))],
        out_specs=pl.BlockSpec((tt, tc), lambda t, j: (t, j)),
        out_shape=jax.ShapeDtypeStruct((T, n), f32),
        scratch_shapes=[pltpu.VMEM((tt + 16, tc), f32)],
        compiler_params=_cparams("parallel", "parallel"),
        name="conv_silu",
    )(zx, zx, zx, conv_w, conv_b)


def _ssd_prep_kernel(raw_ref, bias_ref, alog_ref, dt_ref, ac_ref, act_ref):
    L = SSD_CHUNK
    nh = raw_ref.shape[1] // 2
    hg = nh // SSD_GROUPS
    dt = _softplus(raw_ref[...] + bias_ref[...])
    dta = dt * (-jnp.exp(alog_ref[...]))
    r = lax.broadcasted_iota(i32, (L, L), 0)
    c = lax.broadcasted_iota(i32, (L, L), 1)
    hi = lax.Precision.HIGHEST
    acs = [jnp.dot((c <= r).astype(f32), dta[:, :nh], precision=hi, preferred_element_type=f32),
           jnp.dot((c >= r).astype(f32), dta[:, nh:], precision=hi, preferred_element_type=f32)]
    for d in range(2):
        act = acs[d].T
        for g in range(SSD_GROUPS):
            dt_ref[d, g] = dt[:, d * nh + g * hg:d * nh + (g + 1) * hg]
            ac_ref[d, g] = acs[d][:, g * hg:(g + 1) * hg]
            act_ref[d, g] = act[g * hg:(g + 1) * hg, :]


def ssd_prep(dt_raw, dt_bias, a_log):
    T, nh2 = dt_raw.shape
    nh = nh2 // 2
    hg = nh // SSD_GROUPS
    nc = T // SSD_CHUNK
    vec = pl.BlockSpec((1, nh2), lambda c: (0, 0))
    return pl.pallas_call(
        _ssd_prep_kernel,
        grid=(nc,),
        in_specs=[pl.BlockSpec((SSD_CHUNK, nh2), lambda c: (c, 0)), vec, vec],
        out_specs=[pl.BlockSpec((2, SSD_GROUPS, SSD_CHUNK, hg), lambda c: (0, 0, c, 0)),
                   pl.BlockSpec((2, SSD_GROUPS, SSD_CHUNK, hg), lambda c: (0, 0, c, 0)),
                   pl.BlockSpec((2, SSD_GROUPS, None, hg, SSD_CHUNK), lambda c: (0, 0, c, 0, 0))],
        out_shape=[jax.ShapeDtypeStruct((2, SSD_GROUPS, T, hg), f32),
                   jax.ShapeDtypeStruct((2, SSD_GROUPS, T, hg), f32),
                   jax.ShapeDtypeStruct((2, SSD_GROUPS, nc, hg, SSD_CHUNK), f32)],
        compiler_params=_cparams("parallel"),
        name="ssd_prep",
    )(dt_raw, dt_bias, a_log)


def _ssd_scan_kernel(*refs, reverse, final, hg):
    if final:
        (xs_ref, b_ref, c_ref, dt_ref, ac_ref, act_ref, s0_ref, yf_ref, z_ref, dsk_ref, ng_ref,
         out_ref, st_ref, yz_s) = refs
    else:
        (xs_ref, b_ref, c_ref, dt_ref, ac_ref, act_ref, s0_ref, out_ref, st_ref) = refs
    L = SSD_CHUNK
    P = SSD_HEADDIM
    ci = pl.program_id(0)
    g = pl.program_id(1)

    @pl.when(ci == 0)
    def _():
        st_ref[g] = s0_ref[g]

    xs = xs_ref[...]
    dt16 = dt_ref[...]
    ac16 = ac_ref[...]
    act = act_ref[...]
    expand = (lax.broadcasted_iota(i32, (hg, hg * P), 0)
              == lax.broadcasted_iota(i32, (hg, hg * P), 1) // P).astype(f32)
    dtx = jnp.dot(dt16, expand, precision=lax.Precision.HIGHEST, preferred_element_type=f32)
    xq = (xs * dtx).astype(bf16)
    bm = b_ref[...]
    cm = c_ref[...]
    cb = lax.dot_general(cm.astype(bf16), bm.astype(bf16), (((1,), (1,)), ((), ())), preferred_element_type=f32)
    bt = bm.T
    li = lax.broadcasted_iota(i32, (L, L), 0)
    si = lax.broadcasted_iota(i32, (L, L), 1)
    mask = (li <= si) if reverse else (li >= si)
    last = 0 if reverse else L - 1
    st = st_ref[g]
    stb = st.astype(bf16)
    ys = []
    new_st = []
    for hh in range(hg):
        hs = slice(hh * P, (hh + 1) * P)
        col = ac16[:, hh:hh + 1]
        row = act[hh:hh + 1, :]
        gm = (jnp.exp(jnp.where(mask, col - row, -jnp.inf)) * cb).astype(bf16)
        ce = (cm * jnp.exp(col)).astype(bf16)
        xqh = xq[:, hs]
        ys.append(jnp.dot(gm, xqh, preferred_element_type=f32)
                  + jnp.dot(ce, stb[:, hs], preferred_element_type=f32))
        tot = row[:, last:last + 1]
        bdt = (bt * jnp.exp(tot - row)).astype(bf16)
        new_st.append(st[:, hs] * jnp.exp(tot) + jnp.dot(bdt, xqh, preferred_element_type=f32))
    y = jnp.concatenate(ys, axis=1)
    st_ref[g] = jnp.concatenate(new_st, axis=1)
    if not final:
        out_ref[...] = y
        return
    yz_s[g] = (yf_ref[...] + y + dsk_ref[...] * xs) * _silu(z_ref[...])

    @pl.when(g == SSD_GROUPS - 1)
    def _():
        w = hg * P
        ss = None
        for k in range(SSD_GROUPS):
            v = yz_s[k]
            s = jnp.sum(v * v, axis=-1, keepdims=True)
            ss = s if ss is None else ss + s
        rinv = lax.rsqrt(ss * (1.0 / (w * SSD_GROUPS)) + EPS)
        for k in range(SSD_GROUPS):
            out_ref[:, k * w:(k + 1) * w] = ((yz_s[k] * rinv) * ng_ref[:, k * w:(k + 1) * w]).astype(out_ref.dtype)


def ssd_scan(xbc, dtg, acg, act, s0, fin=None, *, reverse):
    T = xbc.shape[0]
    L = SSD_CHUNK
    nc = T // L
    hg = dtg.shape[-1]
    w = hg * SSD_HEADDIM
    d_inner = w * SSD_GROUPS
    nb0 = d_inner // SSD_STATE
    final = fin is not None
    cix = (lambda c: nc - 1 - c) if reverse else (lambda c: c)
    full_state = pl.BlockSpec(s0.shape, lambda c, g: (0, 0, 0))
    in_specs = [
        pl.BlockSpec((L, w), lambda c, g: (cix(c), g)),
        pl.BlockSpec((L, SSD_STATE), lambda c, g: (cix(c), nb0 + g)),
        pl.BlockSpec((L, SSD_STATE), lambda c, g: (cix(c), nb0 + SSD_GROUPS + g)),
        pl.BlockSpec((None, L, hg), lambda c, g: (g, cix(c), 0)),
        pl.BlockSpec((None, L, hg), lambda c, g: (g, cix(c), 0)),
        pl.BlockSpec((None, None, hg, L), lambda c, g: (g, cix(c), 0, 0)),
        full_state,
    ]
    args = [xbc, xbc, xbc, dtg, acg, act, s0]
    scratch = []
    if final:
        yf, zx, dsk, ng = fin
        in_specs += [pl.BlockSpec((L, w), lambda c, g: (cix(c), g)),
                     pl.BlockSpec((L, w), lambda c, g: (cix(c), g)),
                     pl.BlockSpec((1, w), lambda c, g: (0, g)),
                     pl.BlockSpec((1, d_inner), lambda c, g: (0, 0))]
        args += [yf, zx, dsk, ng]
        out_spec = pl.BlockSpec((L, d_inner), lambda c, g: (cix(c), 0))
        out_shape = jax.ShapeDtypeStruct((T, d_inner), bf16)
        scratch = [pltpu.VMEM((SSD_GROUPS, L, w), f32)]
    else:
        out_spec = pl.BlockSpec((L, w), lambda c, g: (cix(c), g))
        out_shape = jax.ShapeDtypeStruct((T, d_inner), f32)
    return pl.pallas_call(
        functools.partial(_ssd_scan_kernel, reverse=reverse, final=final, hg=hg),
        grid=(nc, SSD_GROUPS),
        in_specs=in_specs,
        out_specs=[out_spec, full_state],
        out_shape=[out_shape, jax.ShapeDtypeStruct(s0.shape, f32)],
        scratch_shapes=scratch,
        compiler_params=_cparams("arbitrary", "arbitrary"),
        name="ssd_scan_bwd" if reverse else "ssd_scan_fwd",
    )(*args)


def ssd_mixer(h_args_l, h_args_c, p):
    d_inner = p["norm_g"].shape[1]
    hg = d_inner // SSD_HEADDIM // SSD_GROUPS
    s_zero = jnp.zeros((SSD_GROUPS, SSD_STATE, hg * SSD_HEADDIM), f32)

    def prep(h_args):
        zx = norm_mod_matmul(*h_args, p["w_zx"])
        dt_raw = norm_mod_matmul(*h_args, p["w_dt"], tn=p["w_dt"].shape[1])
        xbc = conv_silu(zx, p["conv_w"], p["conv_b"], d_inner)
        return (zx, xbc) + tuple(ssd_prep(dt_raw, p["dt_bias"], p["a_log"]))

    zx_c, xbc_c, dt_c, ac_c, act_c = prep(h_args_c)
    zx_l, xbc_l, dt_l, ac_l, act_l = prep(h_args_l)
    yf_c, s_c = ssd_scan(xbc_c, dt_c[0], ac_c[0], act_c[0], s_zero, reverse=False)
    yf_l, _ = ssd_scan(xbc_l, dt_l[0], ac_l[0], act_l[0], s_c, reverse=False)
    y_c, s_c = ssd_scan(xbc_c, dt_c[1], ac_c[1], act_c[1], s_zero, (yf_c, zx_c, p["d_skip"], p["norm_g"]), reverse=True)
    y_l, _ = ssd_scan(xbc_l, dt_l[1], ac_l[1], act_l[1], s_c, (yf_l, zx_l, p["d_skip"], p["norm_g"]), reverse=True)
    return y_l, y_c


def _rope_table_kernel(cos_ref, sin_ref, *, tt):
    half = HEAD_DIM // 2
    t = lax.broadcasted_iota(i32, (tt, HEAD_DIM), 0) + pl.program_id(0) * tt
    lane = lax.broadcasted_iota(i32, (tt, HEAD_DIM), 1)
    pos = jnp.where(lane < half, t // GRID_W, t % GRID_W).astype(f32)
    k = (lane % (half // 2)).astype(f32)
    inv_freq = jnp.exp(k * (-2.0 / half * math.log(ROPE_BASE)))
    ang = pos * inv_freq
    cos_ref[...] = jnp.cos(ang)
    sin_ref[...] = jnp.where(lane % half < half // 2, -1.0, 1.0) * jnp.sin(ang)


def rope_tables(T, *, tt=512):
    tt = _row_tile(T, tt)
    spec = pl.BlockSpec((tt, HEAD_DIM), lambda i: (i, 0))
    return pl.pallas_call(
        functools.partial(_rope_table_kernel, tt=tt),
        grid=(T // tt,),
        in_specs=[],
        out_specs=[spec, spec],
        out_shape=[jax.ShapeDtypeStruct((T, HEAD_DIM), f32)] * 2,
        compiler_params=_cparams("parallel"),
        name="rope_tables",
    )()


def _rope(x, cos, sin):
    n, w = x.shape
    reps = w // HEAD_DIM
    if reps > 1:
        cos = jnp.concatenate([cos] * reps, axis=1)
        sin = jnp.concatenate([sin] * reps, axis=1)
    q = HEAD_DIM // 4
    lane = lax.broadcasted_iota(i32, (n, w), 1)
    partner = jnp.where(lane % (2 * q) < q, pltpu.roll(x, w - q, axis=1), pltpu.roll(x, q, axis=1))
    return x * cos + partner * sin


def _attn_kernel(*refs, band, nq, T):
    if band:
        (q_ref, kp_ref, ko_ref, kn_ref, vp_ref, vo_ref, vn_ref, cp_ref, co_ref, cn_ref, sp_ref, so_ref, sn_ref,
         kc_ref, vc_ref, sink_ref, o_ref) = refs
    else:
        q_ref, kc_ref, vc_ref, sink_ref, o_ref = refs
    nrow = q_ref.shape[0]
    q = q_ref[...]
    if band:
        q = _rope(q, co_ref[...], so_ref[...])
    qs = jnp.concatenate([q[:, j * HEAD_DIM:(j + 1) * HEAD_DIM] for j in range(ATTN_GROUP)], axis=0).astype(bf16)
    kc = kc_ref[...].astype(bf16)
    vc = vc_ref[...].astype(bf16)
    nt = (((1,), (1,)), ((), ()))
    scale = HEAD_DIM ** -0.5
    sink = sink_ref[...]
    s_c = lax.dot_general(qs, kc, nt, preferred_element_type=f32) * scale
    m = jnp.maximum(jnp.max(s_c, axis=1, keepdims=True), sink)
    if band:
        i = pl.program_id(1)
        kb = jnp.concatenate([_rope(kp_ref[...], cp_ref[...], sp_ref[...]),
                              _rope(ko_ref[...], co_ref[...], so_ref[...]),
                              _rope(kn_ref[...], cn_ref[...], sn_ref[...])], axis=0).astype(bf16)
        vb = jnp.concatenate([vp_ref[...], vo_ref[...], vn_ref[...]], axis=0).astype(bf16)
        s_b = lax.dot_general(qs, kb, nt, preferred_element_type=f32) * scale
        shp = s_b.shape
        qi = lax.broadcasted_iota(i32, shp, 0) % nrow + nrow
        ki = lax.broadcasted_iota(i32, shp, 1)
        k_abs = ki + (i - 1) * nrow
        valid = (jnp.abs(qi - ki) <= WINDOW) & (k_abs >= 0) & (k_abs < T)
        s_b = jnp.where(valid, s_b, -jnp.inf)
        m = jnp.maximum(m, jnp.max(s_b, axis=1, keepdims=True))
        p_b = jnp.exp(s_b - m)
    p_c = jnp.exp(s_c - m)
    den = jnp.sum(p_c, axis=1, keepdims=True) + jnp.exp(sink - m)
    if band:
        den = den + jnp.sum(p_b, axis=1, keepdims=True)
    rden = 1.0 / den
    o = jnp.dot((p_c * rden).astype(bf16), vc, preferred_element_type=f32)
    if band:
        o = o + jnp.dot((p_b * rden).astype(bf16), vb, preferred_element_type=f32)
    o_ref[...] = jnp.concatenate([o[j * nrow:(j + 1) * nrow, :] for j in range(ATTN_GROUP)], axis=1).astype(o_ref.dtype)


def window_attention(qkv_l, qkv_c, sink_rows, cos, sin):
    T = qkv_l.shape[0]
    Lc = qkv_c.shape[0]
    nkv = qkv_l.shape[1] // HEAD_DIM // (ATTN_GROUP + 2)
    nh = nkv * ATTN_GROUP
    k0, v0 = nh, nh + nkv
    gw = ATTN_GROUP * HEAD_DIM
    blk = WINDOW
    nq = T // blk
    prv = lambda i: jnp.maximum(i - 1, 0)
    nxt = lambda i: jnp.minimum(i + 1, nq - 1)
    rows = [prv, lambda i: i, nxt]
    kv = lambda c0: [pl.BlockSpec((blk, HEAD_DIM), (lambda g, i, f=f, c0=c0: (f(i), c0 + g))) for f in rows]
    tab = [pl.BlockSpec((blk, HEAD_DIM), (lambda g, i, f=f: (f(i), 0))) for f in rows]
    ctx_specs = lambda: [pl.BlockSpec((Lc, HEAD_DIM), lambda g, i: (0, k0 + g)),
                         pl.BlockSpec((Lc, HEAD_DIM), lambda g, i: (0, v0 + g))]
    o_l = pl.pallas_call(
        functools.partial(_attn_kernel, band=True, nq=nq, T=T),
        grid=(nkv, nq),
        in_specs=[pl.BlockSpec((blk, gw), lambda g, i: (i, g))] + kv(k0) + kv(v0) + tab + tab + ctx_specs()
        + [pl.BlockSpec((None, ATTN_GROUP * blk, 1), lambda g, i: (g, 0, 0))],
        out_specs=pl.BlockSpec((blk, gw), lambda g, i: (i, g)),
        out_shape=jax.ShapeDtypeStruct((T, nh * HEAD_DIM), bf16),
        compiler_params=_cparams("parallel", "parallel"),
        name="window_attention",
    )(qkv_l, *([qkv_l] * 6), cos, cos, cos, sin, sin, sin, qkv_c, qkv_c, sink_rows[0])
    o_c = pl.pallas_call(
        functools.partial(_attn_kernel, band=False, nq=1, T=Lc),
        grid=(nkv, 1),
        in_specs=[pl.BlockSpec((Lc, gw), lambda g, i: (0, g))] + ctx_specs()
        + [pl.BlockSpec((None, ATTN_GROUP * Lc, 1), lambda g, i: (g, 0, 0))],
        out_specs=pl.BlockSpec((Lc, gw), lambda g, i: (0, g)),
        out_shape=jax.ShapeDtypeStruct((Lc, nh * HEAD_DIM), bf16),
        compiler_params=_cparams("parallel", "parallel"),
        name="context_attention",
    )(qkv_c, qkv_c, qkv_c, sink_rows[1])
    return o_l, o_c


def _adaln_kernel(c_ref, dn_ref, up_ref, b_ref, o_ref):
    hi = lax.Precision.HIGHEST
    t = jnp.dot(jax.nn.silu(c_ref[...]), dn_ref[...], precision=hi, preferred_element_type=f32)
    o_ref[...] = jnp.dot(t, up_ref[...], precision=hi, preferred_element_type=f32) + b_ref[...]


def adaln(cond, w_down, w_up, b, *, tn=2048):
    depth, d, r = w_down.shape
    n = w_up.shape[2]
    rows = cond.shape[0]
    return pl.pallas_call(
        _adaln_kernel,
        grid=(depth, n // tn),
        in_specs=[pl.BlockSpec((rows, d), lambda l, j: (0, 0)),
                  pl.BlockSpec((None, d, r), lambda l, j: (l, 0, 0)),
                  pl.BlockSpec((None, r, tn), lambda l, j: (l, 0, j)),
                  pl.BlockSpec((None, 1, tn), lambda l, j: (l, 0, j))],
        out_specs=pl.BlockSpec((None, rows, tn), lambda l, j: (l, 0, j)),
        out_shape=jax.ShapeDtypeStruct((depth, rows, n), f32),
        compiler_params=_cparams("parallel", "parallel"),
        name="adaln",
    )(cond, w_down, w_up, b.reshape(depth, 1, n))


def kernel(x, c, ctx, c_ctx, ada_down, ada_up, ada_b, norm_g, router_w, moe_w_gate, moe_w_up, moe_w_down,
           rg_w_in, rg_conv_w, rg_conv_b, rg_gate_w, rg_gate_b, rg_lambda, rg_w_out,
           ssd_w_in, ssd_conv_w, ssd_conv_b, ssd_dt_bias, ssd_a_log, ssd_d, ssd_norm_g, ssd_w_out,
           attn_w_qkv, attn_sink, attn_w_out):
    assert x.shape[0] == 1 and ctx.shape[0] == 1 and c.shape[0] == 1
    depth = ada_down.shape[0]
    T, d = x.shape[1], x.shape[2]
    Lc = ctx.shape[1]
    xl, xc = x[0], ctx[0]
    cond = jnp.concatenate([c, c_ctx[None, :], jnp.zeros((SUBLANES - 2, d), f32)], axis=0)
    mods = adaln(cond, ada_down, ada_up, ada_b)
    row = lambda v: v.reshape(1, -1)

    for i in range(depth):
        kind, slot = i % N_MIXERS, i // N_MIXERS
        ml = [mods[i, 0:1, k * d:(k + 1) * d] for k in range(N_MOD)]
        mc = [mods[i, 1:2, k * d:(k + 1) * d] for k in range(N_MOD)]
        g = [row(norm_g[i, k]) for k in range(4)]
        last = i == depth - 1
        in_l = (xl, g[0], ml[0], ml[1])
        in_c = (xc, g[0], mc[0], mc[1])
        if kind == 0:
            p = dict(w_in=rg_w_in[slot].astype(bf16), conv_w=rg_conv_w[slot], conv_b=row(rg_conv_b[slot]),
                     gate_w=rg_gate_w[slot].astype(bf16), gate_b=rg_gate_b[slot],
                     lam=rg_lambda[slot].reshape(2, 1, -1))
            y_l, y_c = rglru_mixer(in_l, in_c, p)
            w_out = rg_w_out[slot].astype(bf16)
        elif kind == 1:
            d_inner = ssd_norm_g.shape[1]
            w_in = ssd_w_in[slot]
            n_zx = d_inner + ssd_conv_w.shape[2]
            p = dict(w_zx=w_in[:, :n_zx].astype(bf16), w_dt=w_in[:, n_zx:].astype(bf16),
                     conv_w=ssd_conv_w[slot], conv_b=row(ssd_conv_b[slot]),
                     dt_bias=row(ssd_dt_bias[slot]), a_log=row(ssd_a_log[slot]),
                     d_skip=row(jnp.repeat(ssd_d[slot], SSD_HEADDIM)), norm_g=row(ssd_norm_g[slot]))
            y_l, y_c = ssd_mixer(in_l, in_c, p)
            w_out = ssd_w_out[slot].astype(bf16)
        else:
            w_qkv = attn_w_qkv[slot].astype(bf16)
            qkv_l = norm_mod_matmul(*in_l, w_qkv)
            qkv_c = norm_mod_matmul(*in_c, w_qkv)
            sk = attn_sink[slot].reshape(-1, ATTN_GROUP, 1, 1)
            sink_rows = [jnp.broadcast_to(sk, sk.shape[:2] + (n, 1)).reshape(sk.shape[0], ATTN_GROUP * n, 1)
                         for n in (WINDOW, Lc)]
            cos, sin = rope_tables(T)
            y_l, y_c = window_attention(qkv_l, qkv_c, sink_rows, cos, sin)
            w_out = attn_w_out[slot].astype(bf16)
        mp = dict(router_wt=router_w[i].T,
                  w_gu=jnp.concatenate([moe_w_gate[i], moe_w_up[i]], axis=-1).astype(bf16),
                  w_down=moe_w_down[i].astype(bf16))
        xl = matmul_norm_res(y_l, w_out, g[1], ml[2], xl)
        xl = moe_block(xl, g[2], ml[3], ml[4], g[3], ml[5], mp)
        if not last:
            xc = matmul_norm_res(y_c, w_out, g[1], mc[2], xc)
            xc = moe_block(xc, g[2], mc[3], mc[4], g[3], mc[5], mp)
    return xl[None]
```

```python
import functools
import math

import jax
import jax.numpy as jnp
from jax import lax
from jax.experimental import pallas as pl
from jax.experimental.pallas import tpu as pltpu

f32 = jnp.float32
bf16 = jnp.bfloat16
i32 = jnp.int32

EPS = 1e-6
N_MIXERS = 3
N_MOD = 6
CONV_W = 4
RG_HEADS = 16
RG_C = 8.0
SSD_HEADDIM = 64
SSD_STATE = 128
SSD_GROUPS = 8
SSD_CHUNK = 128
HEAD_DIM = 128
ATTN_GROUP = 4
WINDOW = 128
ATTN_QBLOCKS = 2
ROPE_BASE = 10000.0
GRID_W = 64
N_EXPERTS = 16
EC_CAPACITY = 2

LANES = 128
SUBLANES = 8
VMEM_LIMIT = 56 * 1024 * 1024


def _cparams(*sem):
    return pltpu.CompilerParams(dimension_semantics=sem, vmem_limit_bytes=VMEM_LIMIT)


def _row_tile(m, want):
    t = min(m, want)
    assert m % t == 0
    return t


def _sigmoid(x):
    return 0.5 * jnp.tanh(0.5 * x) + 0.5


def _silu(x):
    return x * _sigmoid(x)


def _rms_mod(x, g, shift, scale):
    y = x * lax.rsqrt(jnp.mean(x * x, axis=-1, keepdims=True) + EPS)
    return (y * g) * (1.0 + scale) + shift


def _nmm_kernel(x_ref, g_ref, sh_ref, sc_ref, w_ref, o_ref, a_ref):
    @pl.when(pl.program_id(1) == 0)
    def _():
        a_ref[...] = _rms_mod(x_ref[...], g_ref[...], sh_ref[...], sc_ref[...]).astype(a_ref.dtype)

    o_ref[...] = jnp.dot(a_ref[...], w_ref[...], preferred_element_type=f32).astype(o_ref.dtype)


def norm_mod_matmul(x, g, shift, scale, w, *, tm=512, tn=1024, out_dtype=f32):
    m, d = x.shape
    n = w.shape[1]
    tm = _row_tile(m, tm)
    tn = _row_tile(n, tn)
    vec = pl.BlockSpec((1, d), lambda i, j: (0, 0))
    return pl.pallas_call(
        _nmm_kernel,
        grid=(m // tm, n // tn),
        in_specs=[pl.BlockSpec((tm, d), lambda i, j: (i, 0)), vec, vec, vec,
                  pl.BlockSpec((d, tn), lambda i, j: (0, j))],
        out_specs=pl.BlockSpec((tm, tn), lambda i, j: (i, j)),
        out_shape=jax.ShapeDtypeStruct((m, n), out_dtype),
        scratch_shapes=[pltpu.VMEM((tm, d), bf16)],
        compiler_params=_cparams("parallel", "arbitrary"),
        name="norm_mod_matmul",
    )(x, g, shift, scale, w)


def _mnr_kernel(a_ref, w_ref, g_ref, gate_ref, r_ref, o_ref, y_ref, ss_ref, *, nj, n_total):
    j = pl.program_id(1)

    @pl.when(j == 0)
    def _():
        ss_ref[...] = jnp.zeros_like(ss_ref)

    @pl.when(j < nj)
    def _():
        y = jnp.dot(a_ref[...], w_ref[...], preferred_element_type=f32)
        y_ref[j] = y
        ss_ref[...] += jnp.sum(y * y, axis=-1, keepdims=True)

    @pl.when(j >= nj)
    def _():
        rinv = lax.rsqrt(ss_ref[...] * (1.0 / n_total) + EPS)
        o_ref[...] = r_ref[...] + gate_ref[...] * ((y_ref[j - nj] * rinv) * g_ref[...])


def matmul_norm_res(a, w, g, gate, resid, *, tn=512):
    m, k = a.shape
    n = w.shape[1]
    tm = _row_tile(m, 4 * 1024 * 1024 // k)
    tn = _row_tile(n, tn)
    nj = n // tn
    ph2 = lambda j: jnp.maximum(j - nj, 0)
    return pl.pallas_call(
        functools.partial(_mnr_kernel, nj=nj, n_total=n),
        grid=(m // tm, 2 * nj),
        in_specs=[pl.BlockSpec((tm, k), lambda i, j: (i, 0)),
                  pl.BlockSpec((k, tn), lambda i, j: (0, jnp.minimum(j, nj - 1))),
                  pl.BlockSpec((1, tn), lambda i, j: (0, ph2(j))),
                  pl.BlockSpec((1, tn), lambda i, j: (0, ph2(j))),
                  pl.BlockSpec((tm, tn), lambda i, j: (i, ph2(j)))],
        out_specs=pl.BlockSpec((tm, tn), lambda i, j: (i, ph2(j))),
        out_shape=jax.ShapeDtypeStruct((m, n), f32),
        scratch_shapes=[pltpu.VMEM((nj, tm, tn), f32), pltpu.VMEM((tm, 1), f32)],
        compiler_params=_cparams("parallel", "arbitrary"),
        name="matmul_norm_res",
    )(a, w, g, gate, resid)


def _softplus(x):
    return jnp.maximum(x, 0.0) + jnp.log1p(jnp.exp(-jnp.abs(x)))


def _rg_sweep_kernel(*refs, reverse, final, nt, tt):
    if final:
        (v_ref, vp_ref, vn_ref, cw_ref, cb_ref, gw_ref, gb_ref, lam_ref, h0_ref, hf_ref, g_ref,
         out_ref, hT_ref, ext_s, a_s, u_s, hs_s, carry_s) = refs
    else:
        (v_ref, vp_ref, vn_ref, cw_ref, cb_ref, gw_ref, gb_ref, lam_ref, h0_ref,
         out_ref, hT_ref, ext_s, a_s, u_s, hs_s, carry_s) = refs
    t = pl.program_id(1)
    ti = nt - 1 - t if reverse else t
    c = v_ref.shape[1]
    s_len = tt // SUBLANES

    @pl.when(t == 0)
    def _():
        carry_s[...] = h0_ref[...]

    ext_s[0:8, :] = jnp.where(ti == 0, 0.0, vp_ref[...])
    ext_s[8:8 + tt, :] = v_ref[...]
    ext_s[8 + tt:16 + tt, :] = jnp.where(ti == nt - 1, 0.0, vn_ref[...])
    vc = cb_ref[...]
    for k in range(CONV_W):
        vc = vc + ext_s[pl.ds(6 + k, tt), :] * cw_ref[k:k + 1, :]

    vb = vc.astype(bf16)
    hw = gw_ref.shape[-1]
    heads = [vb[:, k * hw:(k + 1) * hw] for k in range(c // hw)]
    gate = lambda n: jnp.concatenate([jnp.dot(vh, gw_ref[n, k], preferred_element_type=f32)
                                      for k, vh in enumerate(heads)], axis=1)
    gr = gate(0) + gb_ref[0:1, :]
    gi = gate(1) + gb_ref[1:2, :]
    r = _sigmoid(gr)
    ig = _sigmoid(gi)
    log_a = (-RG_C) * r * _softplus(-lam_ref[...])
    a = jnp.exp(log_a)
    u = jnp.sqrt(-jnp.tanh(log_a) * (a * a + 1.0)) * (ig * vc)

    a_s[...] = a.reshape(SUBLANES, s_len, c)
    u_s[...] = u.reshape(SUBLANES, s_len, c)

    def jj(j):
        return s_len - 1 - j if reverse else j

    def pass1(j, hp):
        h, p = hp
        av = a_s[:, jj(j), :]
        return av * h + u_s[:, jj(j), :], av * p

    hfin, pfin = lax.fori_loop(0, s_len, pass1, (jnp.zeros((SUBLANES, c), f32), jnp.ones((SUBLANES, c), f32)),
                               unroll=8)
    cur = carry_s[...]
    cins = [None] * SUBLANES
    for s in (range(SUBLANES - 1, -1, -1) if reverse else range(SUBLANES)):
        cins[s] = cur
        cur = hfin[s:s + 1, :] + pfin[s:s + 1, :] * cur
    carry_s[...] = cur
    hT_ref[...] = cur

    def pass2(j, h):
        h = a_s[:, jj(j), :] * h + u_s[:, jj(j), :]
        hs_s[:, jj(j), :] = h
        return h

    lax.fori_loop(0, s_len, pass2, jnp.concatenate(cins, axis=0), unroll=8)
    hseq = hs_s[...].reshape(tt, c)
    if final:
        out_ref[...] = ((hf_ref[...] + hseq) * jax.nn.gelu(g_ref[...])).astype(out_ref.dtype)
    else:
        out_ref[...] = hseq


def rg_sweep(gv, conv_w, conv_b, gate_w, gate_b, lam, h0, hf=None, *, reverse, tt=512, heads_per_block=4):
    T, d2 = gv.shape
    d = d2 // 2
    hw = d // RG_HEADS
    c = heads_per_block * hw
    nh = d // c
    tt = _row_tile(T, tt)
    nt = T // tt
    final = hf is not None
    tb = tt // 8
    tix = (lambda t: nt - 1 - t) if reverse else (lambda t: t)
    head_vec = lambda rows: pl.BlockSpec((rows, c), lambda h, t: (0, h))
    in_specs = [
        pl.BlockSpec((tt, c), lambda h, t: (tix(t), nh + h)),
        pl.BlockSpec((8, c), lambda h, t: (jnp.maximum(tix(t) * tb - 1, 0), nh + h)),
        pl.BlockSpec((8, c), lambda h, t: (jnp.minimum((tix(t) + 1) * tb, T // 8 - 1), nh + h)),
        head_vec(CONV_W), head_vec(1),
        pl.BlockSpec((2, heads_per_block, hw, hw), lambda h, t: (0, h, 0, 0)),
        head_vec(2), head_vec(1), head_vec(1),
    ]
    args = [gv, gv, gv, conv_w, conv_b, gate_w, gate_b, lam, h0]
    if final:
        in_specs += [pl.BlockSpec((tt, c), lambda h, t: (tix(t), h)),
                     pl.BlockSpec((tt, c), lambda h, t: (tix(t), h))]
        args += [hf, gv]
    return pl.pallas_call(
        functools.partial(_rg_sweep_kernel, reverse=reverse, final=final, nt=nt, tt=tt),
        grid=(nh, nt),
        in_specs=in_specs,
        out_specs=[pl.BlockSpec((tt, c), lambda h, t: (tix(t), h)), head_vec(1)],
        out_shape=[jax.ShapeDtypeStruct((T, d), bf16 if final else f32), jax.ShapeDtypeStruct((1, d), f32)],
        scratch_shapes=[pltpu.VMEM((tt + 16, c), f32)] + [pltpu.VMEM((SUBLANES, tt // SUBLANES, c), f32)] * 3
        + [pltpu.VMEM((1, c), f32)],
        compiler_params=_cparams("parallel", "arbitrary"),
        name="rg_sweep_bwd" if reverse else "rg_sweep_fwd",
    )(*args)


def rglru_mixer(h_args_l, h_args_c, p):
    w_in = p["w_in"]
    zeros = jnp.zeros((1, w_in.shape[0]), f32)
    gv_c = norm_mod_matmul(*h_args_c, w_in)
    gv_l = norm_mod_matmul(*h_args_l, w_in)
    sw = lambda gv, d, h0, hf, rev: rg_sweep(gv, p["conv_w"], p["conv_b"], p["gate_w"][d], p["gate_b"][d],
                                             p["lam"][d], h0, hf, reverse=rev)
    hf_c, s_c = sw(gv_c, 0, zeros, None, False)
    hf_l, _ = sw(gv_l, 0, s_c, None, False)
    y_c, s_c = sw(gv_c, 1, zeros, hf_c, True)
    y_l, _ = sw(gv_l, 1, s_c, hf_l, True)
    return y_l, y_c


def _slab(first, count, nq):
    start = first * nq
    if not isinstance(start, int):
        start = pl.multiple_of(start, nq)
    return pl.ds(start, count * nq)


def _router_kernel(x_ref, g_ref, sh_ref, sc_ref, rw_ref, h_ref, aff_ref, hs):
    hs[...] = _rms_mod(x_ref[...], g_ref[...], sh_ref[...], sc_ref[...])
    tm, d = hs.shape
    nq = d // LANES
    for q in range(nq):
        h_ref[pl.ds(q, tm, stride=nq), :] = hs[:, q * LANES:(q + 1) * LANES]
    logits = lax.dot_general(rw_ref[...], hs[...], (((1,), (1,)), ((), ())), precision=lax.Precision.HIGHEST,
                             preferred_element_type=f32)
    e = jnp.exp(logits - jnp.max(logits, axis=0, keepdims=True))
    aff_ref[...] = e / jnp.sum(e, axis=0, keepdims=True)


def moe_router(x, g, shift, scale, router_wt, *, tm=512):
    m, d = x.shape
    ne = router_wt.shape[0]
    tm = _row_tile(m, tm)
    vec = pl.BlockSpec((1, d), lambda i: (0, 0))
    return pl.pallas_call(
        _router_kernel,
        grid=(m // tm,),
        in_specs=[pl.BlockSpec((tm, d), lambda i: (i, 0)), vec, vec, vec,
                  pl.BlockSpec((ne, d), lambda i: (0, 0))],
        out_specs=[pl.BlockSpec((tm * (d // LANES), LANES), lambda i: (i, 0)),
                   pl.BlockSpec((ne, tm), lambda i: (0, i))],
        out_shape=[jax.ShapeDtypeStruct((m * (d // LANES), LANES), f32),
                   jax.ShapeDtypeStruct((ne, m), f32)],
        scratch_shapes=[pltpu.VMEM((tm, d), f32)],
        compiler_params=_cparams("parallel"),
        name="moe_router",
    )(x, g, shift, scale, router_wt)


def _select_kernel(aff_ref, idx_ref, gate_ref, off_ref, num_ref, cnt_s, *, cap, jb):
    nb = aff_ref.shape[0]
    hi = lax.Precision.HIGHEST
    aff = aff_ref[...]
    keys = pltpu.bitcast(aff, i32)

    def total(m):
        return jnp.sum(jnp.sum(m.astype(i32), axis=0, keepdims=True), axis=1, keepdims=True)

    def bs(i, thr):
        cand = thr | (jnp.int32(1) << (30 - i))
        return jnp.where(total(keys >= cand) >= cap, cand, thr)

    thr = lax.fori_loop(0, 31, bs, jnp.zeros((1, 1), i32))
    gt = keys > thr
    eq = keys == thr
    need = (cap - total(gt)).astype(f32)

    tri = (lax.broadcasted_iota(i32, (LANES, LANES), 0) <= lax.broadcasted_iota(i32, (LANES, LANES), 1)).astype(bf16)
    bi = lax.broadcasted_iota(i32, (nb, nb), 0)
    bj = lax.broadcasted_iota(i32, (nb, nb), 1)

    def block_counts(m):
        within = jnp.dot(m.astype(f32).astype(bf16), tri, preferred_element_type=f32)
        rows = jnp.broadcast_to(within[:, LANES - 1:LANES], (nb, LANES))
        before = jnp.dot((bj < bi).astype(f32), rows, precision=hi, preferred_element_type=f32)
        return within, rows, before

    w_eq, _, b_eq = block_counts(eq)
    sel = gt | (eq & (b_eq + w_eq - eq.astype(f32) < need))
    w_sel, rows, before = block_counts(sel)
    cnt_s[...] = jnp.where(sel, before + w_sel, 0.0)
    off_ref[...] = before[:, 0:1].astype(i32)
    num_ref[...] = rows[:, 0:1].astype(i32)

    sel_b = sel.astype(f32).astype(bf16)
    per_block = lax.dot_general(jnp.ones((SUBLANES, LANES), bf16), sel_b, (((1,), (1,)), ((), ())),
                                preferred_element_type=f32)
    cum_end = jnp.dot(per_block, (bi <= bj).astype(f32), precision=hi, preferred_element_type=f32)[0:1, :]
    lane_b = lax.broadcasted_iota(i32, (jb, nb), 1).astype(f32)
    lane_t = lax.broadcasted_iota(i32, (jb, LANES), 1).astype(f32)
    row = lax.broadcasted_iota(i32, (jb, 1), 0).astype(f32)

    def group(r, carry):
        j0 = r * jb
        slot = row + lax.convert_element_type(j0, f32)
        blk = jnp.sum((cum_end <= slot).astype(f32), axis=1, keepdims=True)
        onehot = (lane_b == blk).astype(f32)
        cnt_rows = jnp.dot(onehot, cnt_s[...], precision=hi, preferred_element_type=f32)
        aff_rows = jnp.dot(onehot, aff_ref[...], precision=hi, preferred_element_type=f32)
        hit = cnt_rows == slot + 1.0
        tok = blk * LANES + jnp.sum(jnp.where(hit, lane_t, 0.0), axis=1, keepdims=True)
        idx_ref[pl.ds(j0, jb), :] = tok.astype(i32)
        gate_ref[pl.ds(j0, jb), :] = jnp.sum(jnp.where(hit, aff_rows, 0.0), axis=1, keepdims=True)
        return carry

    lax.fori_loop(0, cap // jb, group, 0)


def moe_select(aff_t, cap):
    ne, T = aff_t.shape
    jb = min(cap, LANES)
    tile = LANES * LANES
    tp = -(-T // tile) * tile
    aff3 = jnp.pad(aff_t, ((0, 0), (0, tp - T))).reshape(ne, tp // LANES, LANES)
    nb = tp // LANES
    slots = pl.BlockSpec((None, cap, 1), lambda e: (e, 0, 0))
    blocks = pl.BlockSpec((None, nb, 1), lambda e: (e, 0, 0))
    return pl.pallas_call(
        functools.partial(_select_kernel, cap=cap, jb=jb),
        grid=(ne,),
        in_specs=[pl.BlockSpec((None, nb, LANES), lambda e: (e, 0, 0))],
        out_specs=[slots, slots, blocks, blocks],
        out_shape=[jax.ShapeDtypeStruct((ne, cap, 1), i32), jax.ShapeDtypeStruct((ne, cap, 1), f32),
                   jax.ShapeDtypeStruct((ne, nb, 1), i32), jax.ShapeDtypeStruct((ne, nb, 1), i32)],
        scratch_shapes=[pltpu.VMEM((nb, LANES), f32)],
        compiler_params=_cparams("parallel"),
        name="moe_select",
    )(aff3)


def _expert_kernel(idx_ref, gate_ref, h_hbm, wgu_ref, wd_ref, ye_ref, xbuf, sem, *, rows, nsteps):
    step = pl.program_id(0) * pl.num_programs(1) + pl.program_id(1)
    slot = step % 2
    d = wgu_ref.shape[0]
    nq = d // LANES

    def gather(stp, slt):
        def body(r, c):
            t = idx_ref[stp * rows + r]
            pltpu.make_async_copy(h_hbm.at[_slab(t, 1, nq)], xbuf.at[slt, _slab(r, 1, nq)], sem.at[slt]).start()
            return c

        lax.fori_loop(0, rows, body, 0, unroll=8)

    @pl.when(step == 0)
    def _():
        gather(step, slot)

    @pl.when(step + 1 < nsteps)
    def _():
        gather(step + 1, 1 - slot)

    pltpu.make_async_copy(h_hbm.at[pl.ds(0, rows * nq)], xbuf.at[slot], sem.at[slot]).wait()
    x = jnp.concatenate([xbuf[slot, pl.ds(q, rows, stride=nq), :].astype(bf16) for q in range(nq)], axis=1)
    gu = jnp.dot(x, wgu_ref[...], preferred_element_type=f32)
    ff = gu.shape[1] // 2
    hid = (_silu(gu[:, :ff]) * gu[:, ff:]).astype(bf16)
    gate = gate_ref[...]
    nw = 4 * LANES
    for c in range(d // nw):
        y = jnp.dot(hid, wd_ref[:, c * nw:(c + 1) * nw], preferred_element_type=f32) * gate
        for i in range(nw // LANES):
            ye_ref[pl.ds(c * (nw // LANES) + i, rows, stride=nq), :] = y[:, i * LANES:(i + 1) * LANES]


def moe_experts(idx_flat, gate, h2, w_gu, w_down, *, rows=256):
    ne, cap, _ = gate.shape
    d, ff = w_down.shape[2], w_down.shape[1]
    nq = d // LANES
    rows = _row_tile(cap, rows)
    nblk = cap // rows
    return pl.pallas_call(
        functools.partial(_expert_kernel, rows=rows, nsteps=ne * nblk),
        grid_spec=pltpu.PrefetchScalarGridSpec(
            num_scalar_prefetch=1,
            grid=(ne, nblk),
            in_specs=[pl.BlockSpec((None, rows, 1), lambda e, j, idx: (e, j, 0)),
                      pl.BlockSpec(memory_space=pl.ANY),
                      pl.BlockSpec((None, d, 2 * ff), lambda e, j, idx: (e, 0, 0), pipeline_mode=pl.Buffered(1)),
                      pl.BlockSpec((None, ff, d), lambda e, j, idx: (e, 0, 0), pipeline_mode=pl.Buffered(1))],
            out_specs=pl.BlockSpec((rows * nq, LANES), lambda e, j, idx: (e * nblk + j, 0)),
            scratch_shapes=[pltpu.VMEM((2, rows * nq, LANES), f32), pltpu.SemaphoreType.DMA((2,))]),
        out_shape=jax.ShapeDtypeStruct((ne * cap * nq, LANES), f32),
        compiler_params=_cparams("arbitrary", "arbitrary"),
        name="moe_experts",
    )(idx_flat, gate, h2, w_gu, w_down)


COMBINE_WINDOW_LOG2 = 5
COMBINE_WINDOW = 1 << COMBINE_WINDOW_LOG2


def _combine_kernel(idx_ref, off_ref, num_ref, ye_hbm, x_ref, g_ref, gate_ref, o_ref, stage, extra, acc, sem,
                    *, cap, ne, nbk):
    b = pl.program_id(0)
    slot = b % 2
    tb, d = x_ref.shape
    nq = d // LANES
    win = COMBINE_WINDOW

    def window_start(off):
        return jnp.minimum(off, cap - win)

    def start_windows(bb, slt):
        for e in range(ne):
            s0 = window_start(off_ref[e * nbk + bb])
            pltpu.make_async_copy(ye_hbm.at[_slab(e * cap + s0, win, nq)], stage.at[slt, _slab(e * win, win, nq)],
                                  sem.at[slt]).start()

    @pl.when(b == 0)
    def _():
        start_windows(b, slot)

    @pl.when(b + 1 < nbk)
    def _():
        start_windows(b + 1, 1 - slot)

    pltpu.make_async_copy(ye_hbm.at[_slab(0, ne * win, nq)], stage.at[slot], sem.at[slot]).wait()

    acc[...] = jnp.zeros_like(acc)
    for e in range(ne):
        off = off_ref[e * nbk + b]
        n = num_ref[e * nbk + b]
        s0 = window_start(off)
        n_win = jnp.minimum(n, s0 + win - off)

        def add_row(j, carry, e=e, s0=s0):
            t = idx_ref[e * cap + j] - b * tb
            acc[_slab(t, 1, nq), :] = acc[_slab(t, 1, nq), :] + stage[slot, _slab(e * win + (j - s0), 1, nq), :]
            return carry

        lax.fori_loop(off, off + n_win, add_row, 0)

        def overflow(c, carry, e=e, off=off, n=n, n_win=n_win):
            j0 = off + n_win + c * win
            s1 = window_start(j0)
            cp = pltpu.make_async_copy(ye_hbm.at[_slab(e * cap + s1, win, nq)], extra, sem.at[2])
            cp.start()
            cp.wait()

            def add_extra(j, carry2):
                t = idx_ref[e * cap + j] - b * tb
                acc[_slab(t, 1, nq), :] = acc[_slab(t, 1, nq), :] + extra[_slab(j - s1, 1, nq), :]
                return carry2

            lax.fori_loop(j0, jnp.minimum(j0 + win, off + n), add_extra, 0)
            return carry

        lax.fori_loop(0, lax.shift_right_logical(n - n_win + win - 1, COMBINE_WINDOW_LOG2), overflow, 0)

    sq = jnp.zeros((tb, LANES), f32)
    for q in range(nq):
        part = acc[pl.ds(q, tb, stride=nq), :]
        sq = sq + part * part
    rinv = lax.rsqrt(jnp.sum(sq, axis=-1, keepdims=True) * (1.0 / d) + EPS)
    for q in range(nq):
        sl = slice(q * LANES, (q + 1) * LANES)
        part = acc[pl.ds(q, tb, stride=nq), :]
        o_ref[:, sl] = x_ref[:, sl] + gate_ref[:, sl] * ((part * rinv) * g_ref[:, sl])


def moe_combine(idx_flat, off, num, ye, x, g, gate, *, cap):
    T, d = x.shape
    nq = d // LANES
    tb = LANES
    nbk = T // tb
    ne = off.shape[0] // nbk
    win = COMBINE_WINDOW
    assert cap >= win
    vec = pl.BlockSpec((1, d), lambda i, *_: (0, 0))
    return pl.pallas_call(
        functools.partial(_combine_kernel, cap=cap, ne=ne, nbk=nbk),
        grid_spec=pltpu.PrefetchScalarGridSpec(
            num_scalar_prefetch=3,
            grid=(nbk,),
            in_specs=[pl.BlockSpec(memory_space=pl.ANY),
                      pl.BlockSpec((tb, d), lambda i, *_: (i, 0)), vec, vec],
            out_specs=pl.BlockSpec((tb, d), lambda i, *_: (i, 0)),
            scratch_shapes=[pltpu.VMEM((2, ne * win * nq, LANES), f32), pltpu.VMEM((win * nq, LANES), f32),
                            pltpu.VMEM((tb * nq, LANES), f32), pltpu.SemaphoreType.DMA((3,))]),
        out_shape=jax.ShapeDtypeStruct((T, d), f32),
        compiler_params=_cparams("arbitrary"),
        name="moe_combine",
    )(idx_flat, off, num, ye, x, g, gate)


def moe_block(x, g_in, shift, scale, g_out, gate, p):
    T = x.shape[0]
    cap = EC_CAPACITY * T // N_EXPERTS
    nbk = T // LANES
    h3, aff_t = moe_router(x, g_in, shift, scale, p["router_wt"])
    idx, gates, off, num = moe_select(aff_t, cap)
    idx_flat = idx.reshape(-1)
    ye = moe_experts(idx_flat, gates, h3, p["w_gu"], p["w_down"], rows=min(cap, 512))
    return moe_combine(idx_flat, off[:, :nbk, 0].reshape(-1), num[:, :nbk, 0].reshape(-1), ye, x, g_out, gate, cap=cap)


def _conv_silu_kernel(x_ref, xp_ref, xn_ref, cw_ref, cb_ref, o_ref, ext_s, *, nt, tt):
    ti = pl.program_id(0)
    ext_s[0:8, :] = jnp.where(ti == 0, 0.0, xp_ref[...])
    ext_s[8:8 + tt, :] = x_ref[...]
    ext_s[8 + tt:16 + tt, :] = jnp.where(ti == nt - 1, 0.0, xn_ref[...])
    acc = cb_ref[...]
    for k in range(CONV_W):
        acc = acc + ext_s[pl.ds(6 + k, tt), :] * cw_ref[k:k + 1, :]
    o_ref[...] = _silu(acc)


def conv_silu(zx, conv_w, conv_b, col0, *, tt=512, tc=1024):
    T = zx.shape[0]
    n = conv_w.shape[1]
    tt = _row_tile(T, tt)
    nt = T // tt
    tb = tt // 8
    cb0 = col0 // tc
    return pl.pallas_call(
        functools.partial(_conv_silu_kernel, nt=nt, tt=tt),
        grid=(nt, n // tc),
        in_specs=[pl.BlockSpec((tt, tc), lambda t, j: (t, cb0 + j)),
                  pl.BlockSpec((8, tc), lambda t, j: (jnp.maximum(t * tb - 1, 0), cb0 + j)),
                  pl.BlockSpec((8, tc), lambda t, j: (jnp.minimum((t + 1) * tb, T // 8 - 1), cb0 + j)),
                  pl.BlockSpec((CONV_W, tc), lambda t, j: (0, j)),
                  pl.BlockSpec((1, tc), lambda t, j: (0, j))],
        out_specs=pl.BlockSpec((tt, tc), lambda t, j: (t, j)),
        out_shape=jax.ShapeDtypeStruct((T, n), f32),
        scratch_shapes=[pltpu.VMEM((tt + 16, tc), f32)],
        compiler_params=_cparams("parallel", "parallel"),
        name="conv_silu",
    )(zx, zx, zx, conv_w, conv_b)


def _ssd_prep_kernel(raw_ref, bias_ref, alog_ref, dt_ref, ac_ref, act_ref):
    L = SSD_CHUNK
    nh = raw_ref.shape[1] // 2
    hg = nh // SSD_GROUPS
    dt = _softplus(raw_ref[...] + bias_ref[...])
    dta = dt * (-jnp.exp(alog_ref[...]))
    r = lax.broadcasted_iota(i32, (L, L), 0)
    c = lax.broadcasted_iota(i32, (L, L), 1)
    hi = lax.Precision.HIGHEST
    acs = [jnp.dot((c <= r).astype(f32), dta[:, :nh], precision=hi, preferred_element_type=f32),
           jnp.dot((c >= r).astype(f32), dta[:, nh:], precision=hi, preferred_element_type=f32)]
    for d in range(2):
        act = acs[d].T
        for g in range(SSD_GROUPS):
            dt_ref[d, g] = dt[:, d * nh + g * hg:d * nh + (g + 1) * hg]
            ac_ref[d, g] = acs[d][:, g * hg:(g + 1) * hg]
            act_ref[d, g] = act[g * hg:(g + 1) * hg, :]


def ssd_prep(dt_raw, dt_bias, a_log):
    T, nh2 = dt_raw.shape
    nh = nh2 // 2
    hg = nh // SSD_GROUPS
    nc = T // SSD_CHUNK
    vec = pl.BlockSpec((1, nh2), lambda c: (0, 0))
    return pl.pallas_call(
        _ssd_prep_kernel,
        grid=(nc,),
        in_specs=[pl.BlockSpec((SSD_CHUNK, nh2), lambda c: (c, 0)), vec, vec],
        out_specs=[pl.BlockSpec((2, SSD_GROUPS, SSD_CHUNK, hg), lambda c: (0, 0, c, 0)),
                   pl.BlockSpec((2, SSD_GROUPS, SSD_CHUNK, hg), lambda c: (0, 0, c, 0)),
                   pl.BlockSpec((2, SSD_GROUPS, None, hg, SSD_CHUNK), lambda c: (0, 0, c, 0, 0))],
        out_shape=[jax.ShapeDtypeStruct((2, SSD_GROUPS, T, hg), f32),
                   jax.ShapeDtypeStruct((2, SSD_GROUPS, T, hg), f32),
                   jax.ShapeDtypeStruct((2, SSD_GROUPS, nc, hg, SSD_CHUNK), f32)],
        compiler_params=_cparams("parallel"),
        name="ssd_prep",
    )(dt_raw, dt_bias, a_log)


def _ssd_scan_kernel(*refs, reverse, final, hg):
    if final:
        (xs_ref, b_ref, c_ref, dt_ref, ac_ref, act_ref, s0_ref, yf_ref, z_ref, dsk_ref, ng_ref,
         out_ref, st_ref, yz_s) = refs
    else:
        (xs_ref, b_ref, c_ref, dt_ref, ac_ref, act_ref, s0_ref, out_ref, st_ref) = refs
    L = SSD_CHUNK
    P = SSD_HEADDIM
    ci = pl.program_id(0)
    g = pl.program_id(1)

    @pl.when(ci == 0)
    def _():
        st_ref[g] = s0_ref[g]

    xs = xs_ref[...]
    dt16 = dt_ref[...]
    ac16 = ac_ref[...]
    act = act_ref[...]
    expand = (lax.broadcasted_iota(i32, (hg, hg * P), 0)
              == lax.broadcasted_iota(i32, (hg, hg * P), 1) // P).astype(f32)
    dtx = jnp.dot(dt16, expand, precision=lax.Precision.HIGHEST, preferred_element_type=f32)
    xq = (xs * dtx).astype(bf16)
    bm = b_ref[...]
    cm = c_ref[...]
    cb = lax.dot_general(cm.astype(bf16), bm.astype(bf16), (((1,), (1,)), ((), ())), preferred_element_type=f32)
    bt = bm.T
    li = lax.broadcasted_iota(i32, (L, L), 0)
    si = lax.broadcasted_iota(i32, (L, L), 1)
    mask = (li <= si) if reverse else (li >= si)
    last = 0 if reverse else L - 1
    st = st_ref[g]
    stb = st.astype(bf16)
    ys = []
    new_st = []
    for hh in range(hg):
        hs = slice(hh * P, (hh + 1) * P)
        col = ac16[:, hh:hh + 1]
        row = act[hh:hh + 1, :]
        gm = (jnp.exp(jnp.where(mask, col - row, -jnp.inf)) * cb).astype(bf16)
        ce = (cm * jnp.exp(col)).astype(bf16)
        xqh = xq[:, hs]
        ys.append(jnp.dot(jnp.concatenate([gm, ce], axis=1), jnp.concatenate([xqh, stb[:, hs]], axis=0),
                          preferred_element_type=f32))
        tot = row[:, last:last + 1]
        bdt = (bt * jnp.exp(tot - row)).astype(bf16)
        new_st.append(st[:, hs] * jnp.exp(tot) + jnp.dot(bdt, xqh, preferred_element_type=f32))
    y = jnp.concatenate(ys, axis=1)
    st_ref[g] = jnp.concatenate(new_st, axis=1)
    if not final:
        out_ref[...] = y
        return
    yz_s[g] = (yf_ref[...] + y + dsk_ref[...] * xs) * _silu(z_ref[...])

    @pl.when(g == SSD_GROUPS - 1)
    def _():
        w = hg * P
        ss = None
        for k in range(SSD_GROUPS):
            v = yz_s[k]
            s = jnp.sum(v * v, axis=-1, keepdims=True)
            ss = s if ss is None else ss + s
        rinv = lax.rsqrt(ss * (1.0 / (w * SSD_GROUPS)) + EPS)
        for k in range(SSD_GROUPS):
            out_ref[:, k * w:(k + 1) * w] = ((yz_s[k] * rinv) * ng_ref[:, k * w:(k + 1) * w]).astype(out_ref.dtype)


def ssd_scan(xbc, dtg, acg, act, s0, fin=None, *, reverse):
    T = xbc.shape[0]
    L = SSD_CHUNK
    nc = T // L
    hg = dtg.shape[-1]
    w = hg * SSD_HEADDIM
    d_inner = w * SSD_GROUPS
    nb0 = d_inner // SSD_STATE
    final = fin is not None
    cix = (lambda c: nc - 1 - c) if reverse else (lambda c: c)
    full_state = pl.BlockSpec(s0.shape, lambda c, g: (0, 0, 0))
    in_specs = [
        pl.BlockSpec((L, w), lambda c, g: (cix(c), g)),
        pl.BlockSpec((L, SSD_STATE), lambda c, g: (cix(c), nb0 + g)),
        pl.BlockSpec((L, SSD_STATE), lambda c, g: (cix(c), nb0 + SSD_GROUPS + g)),
        pl.BlockSpec((None, L, hg), lambda c, g: (g, cix(c), 0)),
        pl.BlockSpec((None, L, hg), lambda c, g: (g, cix(c), 0)),
        pl.BlockSpec((None, None, hg, L), lambda c, g: (g, cix(c), 0, 0)),
        full_state,
    ]
    args = [xbc, xbc, xbc, dtg, acg, act, s0]
    scratch = []
    if final:
        yf, zx, dsk, ng = fin
        in_specs += [pl.BlockSpec((L, w), lambda c, g: (cix(c), g)),
                     pl.BlockSpec((L, w), lambda c, g: (cix(c), g)),
                     pl.BlockSpec((1, w), lambda c, g: (0, g)),
                     pl.BlockSpec((1, d_inner), lambda c, g: (0, 0))]
        args += [yf, zx, dsk, ng]
        out_spec = pl.BlockSpec((L, d_inner), lambda c, g: (cix(c), 0))
        out_shape = jax.ShapeDtypeStruct((T, d_inner), bf16)
        scratch = [pltpu.VMEM((SSD_GROUPS, L, w), f32)]
    else:
        out_spec = pl.BlockSpec((L, w), lambda c, g: (cix(c), g))
        out_shape = jax.ShapeDtypeStruct((T, d_inner), f32)
    return pl.pallas_call(
        functools.partial(_ssd_scan_kernel, reverse=reverse, final=final, hg=hg),
        grid=(nc, SSD_GROUPS),
        in_specs=in_specs,
        out_specs=[out_spec, full_state],
        out_shape=[out_shape, jax.ShapeDtypeStruct(s0.shape, f32)],
        scratch_shapes=scratch,
        compiler_params=_cparams("arbitrary", "arbitrary"),
        name="ssd_scan_bwd" if reverse else "ssd_scan_fwd",
    )(*args)


def ssd_mixer(h_args_l, h_args_c, p):
    d_inner = p["norm_g"].shape[1]
    hg = d_inner // SSD_HEADDIM // SSD_GROUPS
    s_zero = jnp.zeros((SSD_GROUPS, SSD_STATE, hg * SSD_HEADDIM), f32)

    def prep(h_args):
        zx = norm_mod_matmul(*h_args, p["w_zx"])
        dt_raw = norm_mod_matmul(*h_args, p["w_dt"], tn=p["w_dt"].shape[1])
        xbc = conv_silu(zx, p["conv_w"], p["conv_b"], d_inner)
        return (zx, xbc) + tuple(ssd_prep(dt_raw, p["dt_bias"], p["a_log"]))

    zx_c, xbc_c, dt_c, ac_c, act_c = prep(h_args_c)
    zx_l, xbc_l, dt_l, ac_l, act_l = prep(h_args_l)
    yf_c, s_c = ssd_scan(xbc_c, dt_c[0], ac_c[0], act_c[0], s_zero, reverse=False)
    yf_l, _ = ssd_scan(xbc_l, dt_l[0], ac_l[0], act_l[0], s_c, reverse=False)
    y_c, s_c = ssd_scan(xbc_c, dt_c[1], ac_c[1], act_c[1], s_zero, (yf_c, zx_c, p["d_skip"], p["norm_g"]), reverse=True)
    y_l, _ = ssd_scan(xbc_l, dt_l[1], ac_l[1], act_l[1], s_c, (yf_l, zx_l, p["d_skip"], p["norm_g"]), reverse=True)
    return y_l, y_c


def _rope_table_kernel(cos_ref, sin_ref, *, tt):
    half = HEAD_DIM // 2
    t = lax.broadcasted_iota(i32, (tt, HEAD_DIM), 0) + pl.program_id(0) * tt
    lane = lax.broadcasted_iota(i32, (tt, HEAD_DIM), 1)
    pos = jnp.where(lane < half, t // GRID_W, t % GRID_W).astype(f32)
    k = (lane % (half // 2)).astype(f32)
    inv_freq = jnp.exp(k * (-2.0 / half * math.log(ROPE_BASE)))
    ang = pos * inv_freq
    cos_ref[...] = jnp.cos(ang)
    sin_ref[...] = jnp.where(lane % half < half // 2, -1.0, 1.0) * jnp.sin(ang)


def rope_tables(T, *, tt=512):
    tt = _row_tile(T, tt)
    spec = pl.BlockSpec((tt, HEAD_DIM), lambda i: (i, 0))
    return pl.pallas_call(
        functools.partial(_rope_table_kernel, tt=tt),
        grid=(T // tt,),
        in_specs=[],
        out_specs=[spec, spec],
        out_shape=[jax.ShapeDtypeStruct((T, HEAD_DIM), f32)] * 2,
        compiler_params=_cparams("parallel"),
        name="rope_tables",
    )()


def _rope(x, cos, sin):
    n, w = x.shape
    reps = w // HEAD_DIM
    if reps > 1:
        cos = jnp.concatenate([cos] * reps, axis=1)
        sin = jnp.concatenate([sin] * reps, axis=1)
    q = HEAD_DIM // 4
    lane = lax.broadcasted_iota(i32, (n, w), 1)
    partner = jnp.where(lane % (2 * q) < q, pltpu.roll(x, w - q, axis=1), pltpu.roll(x, q, axis=1))
    return x * cos + partner * sin


def _attn_kernel(*refs, band, nq, T):
    if band:
        (q_ref, kp_ref, ko_ref, kn_ref, vp_ref, vo_ref, vn_ref, cp_ref, co_ref, cn_ref, sp_ref, so_ref, sn_ref,
         kc_ref, vc_ref, sink_ref, o_ref) = refs
    else:
        q_ref, kc_ref, vc_ref, sink_ref, o_ref = refs
    nrow = q_ref.shape[0]
    q = q_ref[...]
    if band:
        q = _rope(q, co_ref[...], so_ref[...])
    qs = jnp.concatenate([q[:, j * HEAD_DIM:(j + 1) * HEAD_DIM] for j in range(ATTN_GROUP)], axis=0).astype(bf16)
    kc = kc_ref[...].astype(bf16)
    vc = vc_ref[...].astype(bf16)
    nt = (((1,), (1,)), ((), ()))
    scale = HEAD_DIM ** -0.5
    sink = sink_ref[...]
    s_c = lax.dot_general(qs, kc, nt, preferred_element_type=f32) * scale
    m = jnp.maximum(jnp.max(s_c, axis=1, keepdims=True), sink)
    if band:
        i = pl.program_id(1)
        kb = jnp.concatenate([_rope(kp_ref[...], cp_ref[...], sp_ref[...]),
                              _rope(ko_ref[...], co_ref[...], so_ref[...]),
                              _rope(kn_ref[...], cn_ref[...], sn_ref[...])], axis=0).astype(bf16)
        vb = jnp.concatenate([vp_ref[...], vo_ref[...], vn_ref[...]], axis=0).astype(bf16)
        s_b = lax.dot_general(qs, kb, nt, preferred_element_type=f32) * scale
        shp = s_b.shape
        qi = lax.broadcasted_iota(i32, shp, 0) % nrow + WINDOW
        ki = lax.broadcasted_iota(i32, shp, 1)
        k_abs = ki + (i * nrow - WINDOW)
        valid = (jnp.abs(qi - ki) <= WINDOW) & (k_abs >= 0) & (k_abs < T)
        s_b = jnp.where(valid, s_b, -jnp.inf)
        m = jnp.maximum(m, jnp.max(s_b, axis=1, keepdims=True))
        p_b = jnp.exp(s_b - m)
    p_c = jnp.exp(s_c - m)
    den = jnp.sum(p_c, axis=1, keepdims=True) + jnp.exp(sink - m)
    if band:
        den = den + jnp.sum(p_b, axis=1, keepdims=True)
    rden = 1.0 / den
    o = jnp.dot((p_c * rden).astype(bf16), vc, preferred_element_type=f32)
    if band:
        o = o + jnp.dot((p_b * rden).astype(bf16), vb, preferred_element_type=f32)
    o_ref[...] = jnp.concatenate([o[j * nrow:(j + 1) * nrow, :] for j in range(ATTN_GROUP)], axis=1).astype(o_ref.dtype)


def window_attention(qkv_l, qkv_c, sink_rows, cos, sin):
    T = qkv_l.shape[0]
    Lc = qkv_c.shape[0]
    nkv = qkv_l.shape[1] // HEAD_DIM // (ATTN_GROUP + 2)
    nh = nkv * ATTN_GROUP
    k0, v0 = nh, nh + nkv
    gw = ATTN_GROUP * HEAD_DIM
    qb = ATTN_QBLOCKS
    blk = qb * WINDOW
    nq = T // blk
    nw = T // WINDOW
    rows = [(WINDOW, lambda i: jnp.maximum(qb * i - 1, 0)), (blk, lambda i: i),
            (WINDOW, lambda i: jnp.minimum(qb * (i + 1), nw - 1))]
    kv = lambda c0: [pl.BlockSpec((r, HEAD_DIM), (lambda g, i, f=f, c0=c0: (f(i), c0 + g))) for r, f in rows]
    tab = [pl.BlockSpec((r, HEAD_DIM), (lambda g, i, f=f: (f(i), 0))) for r, f in rows]
    ctx_specs = lambda: [pl.BlockSpec((Lc, HEAD_DIM), lambda g, i: (0, k0 + g)),
                         pl.BlockSpec((Lc, HEAD_DIM), lambda g, i: (0, v0 + g))]
    o_l = pl.pallas_call(
        functools.partial(_attn_kernel, band=True, nq=nq, T=T),
        grid=(nkv, nq),
        in_specs=[pl.BlockSpec((blk, gw), lambda g, i: (i, g))] + kv(k0) + kv(v0) + tab + tab + ctx_specs()
        + [pl.BlockSpec((None, ATTN_GROUP * blk, 1), lambda g, i: (g, 0, 0))],
        out_specs=pl.BlockSpec((blk, gw), lambda g, i: (i, g)),
        out_shape=jax.ShapeDtypeStruct((T, nh * HEAD_DIM), bf16),
        compiler_params=_cparams("parallel", "parallel"),
        name="window_attention",
    )(qkv_l, *([qkv_l] * 6), cos, cos, cos, sin, sin, sin, qkv_c, qkv_c, sink_rows[0])
    o_c = pl.pallas_call(
        functools.partial(_attn_kernel, band=False, nq=1, T=Lc),
        grid=(nkv, 1),
        in_specs=[pl.BlockSpec((Lc, gw), lambda g, i: (0, g))] + ctx_specs()
        + [pl.BlockSpec((None, ATTN_GROUP * Lc, 1), lambda g, i: (g, 0, 0))],
        out_specs=pl.BlockSpec((Lc, gw), lambda g, i: (0, g)),
        out_shape=jax.ShapeDtypeStruct((Lc, nh * HEAD_DIM), bf16),
        compiler_params=_cparams("parallel", "parallel"),
        name="context_attention",
    )(qkv_c, qkv_c, qkv_c, sink_rows[1])
    return o_l, o_c


def _adaln_kernel(c_ref, dn_ref, up_ref, b_ref, o_ref):
    hi = lax.Precision.HIGHEST
    t = jnp.dot(jax.nn.silu(c_ref[...]), dn_ref[...], precision=hi, preferred_element_type=f32)
    o_ref[...] = jnp.dot(t, up_ref[...], precision=hi, preferred_element_type=f32) + b_ref[...]


def adaln(cond, w_down, w_up, b, *, tn=2048):
    depth, d, r = w_down.shape
    n = w_up.shape[2]
    rows = cond.shape[0]
    return pl.pallas_call(
        _adaln_kernel,
        grid=(depth, n // tn),
        in_specs=[pl.BlockSpec((rows, d), lambda l, j: (0, 0)),
                  pl.BlockSpec((None, d, r), lambda l, j: (l, 0, 0)),
                  pl.BlockSpec((None, r, tn), lambda l, j: (l, 0, j)),
                  pl.BlockSpec((None, 1, tn), lambda l, j: (l, 0, j))],
        out_specs=pl.BlockSpec((None, rows, tn), lambda l, j: (l, 0, j)),
        out_shape=jax.ShapeDtypeStruct((depth, rows, n), f32),
        compiler_params=_cparams("parallel", "parallel"),
        name="adaln",
    )(cond, w_down, w_up, b.reshape(depth, 1, n))


def kernel(x, c, ctx, c_ctx, ada_down, ada_up, ada_b, norm_g, router_w, moe_w_gate, moe_w_up, moe_w_down,
           rg_w_in, rg_conv_w, rg_conv_b, rg_gate_w, rg_gate_b, rg_lambda, rg_w_out,
           ssd_w_in, ssd_conv_w, ssd_conv_b, ssd_dt_bias, ssd_a_log, ssd_d, ssd_norm_g, ssd_w_out,
           attn_w_qkv, attn_sink, attn_w_out):
    assert x.shape[0] == 1 and ctx.shape[0] == 1 and c.shape[0] == 1
    depth = ada_down.shape[0]
    T, d = x.shape[1], x.shape[2]
    Lc = ctx.shape[1]
    xl, xc = x[0], ctx[0]
    cond = jnp.concatenate([c, c_ctx[None, :], jnp.zeros((SUBLANES - 2, d), f32)], axis=0)
    mods = adaln(cond, ada_down, ada_up, ada_b)
    row = lambda v: v.reshape(1, -1)

    for i in range(depth):
        kind, slot = i % N_MIXERS, i // N_MIXERS
        ml = [mods[i, 0:1, k * d:(k + 1) * d] for k in range(N_MOD)]
        mc = [mods[i, 1:2, k * d:(k + 1) * d] for k in range(N_MOD)]
        g = [row(norm_g[i, k]) for k in range(4)]
        last = i == depth - 1
        in_l = (xl, g[0], ml[0], ml[1])
        in_c = (xc, g[0], mc[0], mc[1])
        if kind == 0:
            p = dict(w_in=rg_w_in[slot].astype(bf16), conv_w=rg_conv_w[slot], conv_b=row(rg_conv_b[slot]),
                     gate_w=rg_gate_w[slot].astype(bf16), gate_b=rg_gate_b[slot],
                     lam=rg_lambda[slot].reshape(2, 1, -1))
            y_l, y_c = rglru_mixer(in_l, in_c, p)
            w_out = rg_w_out[slot].astype(bf16)
        elif kind == 1:
            d_inner = ssd_norm_g.shape[1]
            w_in = ssd_w_in[slot]
            n_zx = d_inner + ssd_conv_w.shape[2]
            p = dict(w_zx=w_in[:, :n_zx].astype(bf16), w_dt=w_in[:, n_zx:].astype(bf16),
                     conv_w=ssd_conv_w[slot], conv_b=row(ssd_conv_b[slot]),
                     dt_bias=row(ssd_dt_bias[slot]), a_log=row(ssd_a_log[slot]),
                     d_skip=row(jnp.repeat(ssd_d[slot], SSD_HEADDIM)), norm_g=row(ssd_norm_g[slot]))
            y_l, y_c = ssd_mixer(in_l, in_c, p)
            w_out = ssd_w_out[slot].astype(bf16)
        else:
            w_qkv = attn_w_qkv[slot].astype(bf16)
            qkv_l = norm_mod_matmul(*in_l, w_qkv)
            qkv_c = norm_mod_matmul(*in_c, w_qkv)
            sk = attn_sink[slot].reshape(-1, ATTN_GROUP, 1, 1)
            sink_rows = [jnp.broadcast_to(sk, sk.shape[:2] + (n, 1)).reshape(sk.shape[0], ATTN_GROUP * n, 1)
                         for n in (ATTN_QBLOCKS * WINDOW, Lc)]
            cos, sin = rope_tables(T)
            y_l, y_c = window_attention(qkv_l, qkv_c, sink_rows, cos, sin)
            w_out = attn_w_out[slot].astype(bf16)
        mp = dict(router_wt=router_w[i].T,
                  w_gu=jnp.concatenate([moe_w_gate[i], moe_w_up[i]], axis=-1).astype(bf16),
                  w_down=moe_w_down[i].astype(bf16))
        xl = matmul_norm_res(y_l, w_out, g[1], ml[2], xl)
        xl = moe_block(xl, g[2], ml[3], ml[4], g[3], ml[5], mp)
        if not last:
            xc = matmul_norm_res(y_c, w_out, g[1], mc[2], xc)
            xc = moe_block(xc, g[2], mc[3], mc[4], g[3], mc[5], mp)
    return xl[None]
```

```python
import functools
import math

import jax
import jax.numpy as jnp
from jax import lax
from jax.experimental import pallas as pl
from jax.experimental.pallas import tpu as pltpu

f32 = jnp.float32
bf16 = jnp.bfloat16
i32 = jnp.int32

EPS = 1e-6
N_MIXERS = 3
N_MOD = 6
CONV_W = 4
RG_HEADS = 16
RG_C = 8.0
SSD_HEADDIM = 64
SSD_STATE = 128
SSD_GROUPS = 8
SSD_CHUNK = 128
HEAD_DIM = 128
ATTN_GROUP = 4
WINDOW = 128
ATTN_QBLOCKS = 2
ROPE_BASE = 10000.0
GRID_W = 64
N_EXPERTS = 16
EC_CAPACITY = 2

LANES = 128
SUBLANES = 8
VMEM_LIMIT = 56 * 1024 * 1024


def _cparams(*sem):
    return pltpu.CompilerParams(dimension_semantics=sem, vmem_limit_bytes=VMEM_LIMIT)


def _row_tile(m, want):
    t = min(m, want)
    assert m % t == 0
    return t


def _sigmoid(x):
    return 0.5 * jnp.tanh(0.5 * x) + 0.5


def _silu(x):
    return x * _sigmoid(x)


def _rms_mod(x, g, shift, scale):
    y = x * lax.rsqrt(jnp.mean(x * x, axis=-1, keepdims=True) + EPS)
    return (y * g) * (1.0 + scale) + shift


CAST_BLOCK_BYTES = 4 * 1024 * 1024


def _cast_kernel(*refs):
    *ins, o_ref = refs
    col = 0
    for r in ins:
        w = r.shape[-1]
        o_ref[:, col:col + w] = r[...].astype(o_ref.dtype)
        col += w


def to_bf16(*ws, col0=0, ncols=None):
    lead = ws[0].shape[:-1]
    xs = [w.reshape(-1, w.shape[-1]) for w in ws]
    m = xs[0].shape[0]
    if len(xs) > 1 or ncols is None:
        assert col0 == 0 and ncols is None
        tcs = [x.shape[1] for x in xs]
        n = sum(tcs)
        nj = 1
    else:
        n = ncols
        tc = 2048 if (n % 2048 == 0 and col0 % 2048 == 0) else n
        assert col0 % tc == 0
        tcs = [tc]
        nj = n // tc
    if nj == 1 and len(xs) == 1 and n % 2048 == 0 and n > 2048:
        tcs, nj = [2048], n // 2048
    want = max(SUBLANES, CAST_BLOCK_BYTES // (4 * sum(tcs)))
    tr = _row_tile(m, 1 << (want.bit_length() - 1))
    c0 = col0 // tcs[0]
    out = pl.pallas_call(
        _cast_kernel,
        grid=(m // tr, nj),
        in_specs=[pl.BlockSpec((tr, tc), lambda i, j: (i, j + c0)) for tc in tcs],
        out_specs=pl.BlockSpec((tr, sum(tcs)), lambda i, j: (i, j)),
        out_shape=jax.ShapeDtypeStruct((m, n), bf16),
        compiler_params=_cparams("parallel", "parallel"),
        name="to_bf16",
    )(*xs)
    return out.reshape(lead + (n,))


def _nmm_kernel(x_ref, g_ref, sh_ref, sc_ref, w_ref, o_ref, a_ref):
    @pl.when(pl.program_id(1) == 0)
    def _():
        a_ref[...] = _rms_mod(x_ref[...], g_ref[...], sh_ref[...], sc_ref[...]).astype(a_ref.dtype)

    o_ref[...] = jnp.dot(a_ref[...], w_ref[...], preferred_element_type=f32).astype(o_ref.dtype)


def norm_mod_matmul(x, g, shift, scale, w, *, tm=512, tn=1024, out_dtype=f32):
    m, d = x.shape
    n = w.shape[1]
    tm = _row_tile(m, tm)
    tn = _row_tile(n, tn)
    vec = pl.BlockSpec((1, d), lambda i, j: (0, 0))
    return pl.pallas_call(
        _nmm_kernel,
        grid=(m // tm, n // tn),
        in_specs=[pl.BlockSpec((tm, d), lambda i, j: (i, 0)), vec, vec, vec,
                  pl.BlockSpec((d, tn), lambda i, j: (0, j))],
        out_specs=pl.BlockSpec((tm, tn), lambda i, j: (i, j)),
        out_shape=jax.ShapeDtypeStruct((m, n), out_dtype),
        scratch_shapes=[pltpu.VMEM((tm, d), bf16)],
        compiler_params=_cparams("parallel", "arbitrary"),
        name="norm_mod_matmul",
    )(x, g, shift, scale, w)


def _mnr_kernel(a_ref, w_ref, g_ref, gate_ref, r_ref, o_ref, y_ref, ss_ref, *, nj, n_total):
    j = pl.program_id(1)

    @pl.when(j == 0)
    def _():
        ss_ref[...] = jnp.zeros_like(ss_ref)

    @pl.when(j < nj)
    def _():
        y = jnp.dot(a_ref[...], w_ref[...], preferred_element_type=f32)
        y_ref[j] = y
        ss_ref[...] += jnp.sum(y * y, axis=-1, keepdims=True)

    @pl.when(j >= nj)
    def _():
        rinv = lax.rsqrt(ss_ref[...] * (1.0 / n_total) + EPS)
        o_ref[...] = r_ref[...] + gate_ref[...] * ((y_ref[j - nj] * rinv) * g_ref[...])


def matmul_norm_res(a, w, g, gate, resid, *, tn=512):
    m, k = a.shape
    n = w.shape[1]
    tm = _row_tile(m, 4 * 1024 * 1024 // k)
    tn = _row_tile(n, tn)
    nj = n // tn
    ph2 = lambda j: jnp.maximum(j - nj, 0)
    return pl.pallas_call(
        functools.partial(_mnr_kernel, nj=nj, n_total=n),
        grid=(m // tm, 2 * nj),
        in_specs=[pl.BlockSpec((tm, k), lambda i, j: (i, 0)),
                  pl.BlockSpec((k, tn), lambda i, j: (0, jnp.minimum(j, nj - 1))),
                  pl.BlockSpec((1, tn), lambda i, j: (0, ph2(j))),
                  pl.BlockSpec((1, tn), lambda i, j: (0, ph2(j))),
                  pl.BlockSpec((tm, tn), lambda i, j: (i, ph2(j)))],
        out_specs=pl.BlockSpec((tm, tn), lambda i, j: (i, ph2(j))),
        out_shape=jax.ShapeDtypeStruct((m, n), f32),
        scratch_shapes=[pltpu.VMEM((nj, tm, tn), f32), pltpu.VMEM((tm, 1), f32)],
        compiler_params=_cparams("parallel", "arbitrary"),
        name="matmul_norm_res",
    )(a, w, g, gate, resid)


def _softplus(x):
    return jnp.maximum(x, 0.0) + jnp.log1p(jnp.exp(-jnp.abs(x)))


def _rg_sweep_kernel(*refs, reverse, final, nt, tt):
    if final:
        (v_ref, vp_ref, vn_ref, cw_ref, cb_ref, gw_ref, gb_ref, lam_ref, h0_ref, hf_ref, g_ref,
         out_ref, hT_ref, ext_s, a_s, u_s, hs_s, carry_s) = refs
    else:
        (v_ref, vp_ref, vn_ref, cw_ref, cb_ref, gw_ref, gb_ref, lam_ref, h0_ref,
         out_ref, hT_ref, ext_s, a_s, u_s, hs_s, carry_s) = refs
    t = pl.program_id(1)
    ti = nt - 1 - t if reverse else t
    c = v_ref.shape[1]
    s_len = tt // SUBLANES

    @pl.when(t == 0)
    def _():
        carry_s[...] = h0_ref[...]

    ext_s[0:8, :] = jnp.where(ti == 0, 0.0, vp_ref[...])
    ext_s[8:8 + tt, :] = v_ref[...]
    ext_s[8 + tt:16 + tt, :] = jnp.where(ti == nt - 1, 0.0, vn_ref[...])
    vc = cb_ref[...]
    for k in range(CONV_W):
        vc = vc + ext_s[pl.ds(6 + k, tt), :] * cw_ref[k:k + 1, :]

    vb = vc.astype(bf16)
    hw = gw_ref.shape[-1]
    heads = [vb[:, k * hw:(k + 1) * hw] for k in range(c // hw)]
    gate = lambda n: jnp.concatenate([jnp.dot(vh, gw_ref[n, k], preferred_element_type=f32)
                                      for k, vh in enumerate(heads)], axis=1)
    gr = gate(0) + gb_ref[0:1, :]
    gi = gate(1) + gb_ref[1:2, :]
    r = _sigmoid(gr)
    ig = _sigmoid(gi)
    log_a = (-RG_C) * r * _softplus(-lam_ref[...])
    a = jnp.exp(log_a)
    u = jnp.sqrt(-jnp.tanh(log_a) * (a * a + 1.0)) * (ig * vc)

    a_s[...] = a.reshape(SUBLANES, s_len, c)
    u_s[...] = u.reshape(SUBLANES, s_len, c)

    def jj(j):
        return s_len - 1 - j if reverse else j

    def pass1(j, hp):
        h, p = hp
        av = a_s[:, jj(j), :]
        return av * h + u_s[:, jj(j), :], av * p

    hfin, pfin = lax.fori_loop(0, s_len, pass1, (jnp.zeros((SUBLANES, c), f32), jnp.ones((SUBLANES, c), f32)),
                               unroll=8)
    cur = carry_s[...]
    cins = [None] * SUBLANES
    for s in (range(SUBLANES - 1, -1, -1) if reverse else range(SUBLANES)):
        cins[s] = cur
        cur = hfin[s:s + 1, :] + pfin[s:s + 1, :] * cur
    carry_s[...] = cur
    hT_ref[...] = cur

    def pass2(j, h):
        h = a_s[:, jj(j), :] * h + u_s[:, jj(j), :]
        hs_s[:, jj(j), :] = h
        return h

    lax.fori_loop(0, s_len, pass2, jnp.concatenate(cins, axis=0), unroll=8)
    hseq = hs_s[...].reshape(tt, c)
    if final:
        out_ref[...] = ((hf_ref[...] + hseq) * jax.nn.gelu(g_ref[...])).astype(out_ref.dtype)
    else:
        out_ref[...] = hseq


def rg_sweep(gv, conv_w, conv_b, gate_w, gate_b, lam, h0, hf=None, *, reverse, tt=512, heads_per_block=4):
    T, d2 = gv.shape
    d = d2 // 2
    hw = d // RG_HEADS
    c = heads_per_block * hw
    nh = d // c
    tt = _row_tile(T, tt)
    nt = T // tt
    final = hf is not None
    tb = tt // 8
    tix = (lambda t: nt - 1 - t) if reverse else (lambda t: t)
    head_vec = lambda rows: pl.BlockSpec((rows, c), lambda h, t: (0, h))
    in_specs = [
        pl.BlockSpec((tt, c), lambda h, t: (tix(t), nh + h)),
        pl.BlockSpec((8, c), lambda h, t: (jnp.maximum(tix(t) * tb - 1, 0), nh + h)),
        pl.BlockSpec((8, c), lambda h, t: (jnp.minimum((tix(t) + 1) * tb, T // 8 - 1), nh + h)),
        head_vec(CONV_W), head_vec(1),
        pl.BlockSpec((2, heads_per_block, hw, hw), lambda h, t: (0, h, 0, 0)),
        head_vec(2), head_vec(1), head_vec(1),
    ]
    args = [gv, gv, gv, conv_w, conv_b, gate_w, gate_b, lam, h0]
    if final:
        in_specs += [pl.BlockSpec((tt, c), lambda h, t: (tix(t), h)),
                     pl.BlockSpec((tt, c), lambda h, t: (tix(t), h))]
        args += [hf, gv]
    return pl.pallas_call(
        functools.partial(_rg_sweep_kernel, reverse=reverse, final=final, nt=nt, tt=tt),
        grid=(nh, nt),
        in_specs=in_specs,
        out_specs=[pl.BlockSpec((tt, c), lambda h, t: (tix(t), h)), head_vec(1)],
        out_shape=[jax.ShapeDtypeStruct((T, d), bf16 if final else f32), jax.ShapeDtypeStruct((1, d), f32)],
        scratch_shapes=[pltpu.VMEM((tt + 16, c), f32)] + [pltpu.VMEM((SUBLANES, tt // SUBLANES, c), f32)] * 3
        + [pltpu.VMEM((1, c), f32)],
        compiler_params=_cparams("parallel", "arbitrary"),
        name="rg_sweep_bwd" if reverse else "rg_sweep_fwd",
    )(*args)


def rglru_mixer(h_args_l, h_args_c, p):
    w_in = p["w_in"]
    zeros = jnp.zeros((1, w_in.shape[0]), f32)
    gv_c = norm_mod_matmul(*h_args_c, w_in)
    gv_l = norm_mod_matmul(*h_args_l, w_in)
    sw = lambda gv, d, h0, hf, rev: rg_sweep(gv, p["conv_w"], p["conv_b"], p["gate_w"][d], p["gate_b"][d],
                                             p["lam"][d], h0, hf, reverse=rev)
    hf_c, s_c = sw(gv_c, 0, zeros, None, False)
    hf_l, _ = sw(gv_l, 0, s_c, None, False)
    y_c, s_c = sw(gv_c, 1, zeros, hf_c, True)
    y_l, _ = sw(gv_l, 1, s_c, hf_l, True)
    return y_l, y_c


def _slab(first, count, nq):
    start = first * nq
    if not isinstance(start, int):
        start = pl.multiple_of(start, nq)
    return pl.ds(start, count * nq)


def _router_kernel(x_ref, g_ref, sh_ref, sc_ref, rw_ref, h_ref, aff_ref):
    h_ref[...] = _rms_mod(x_ref[...], g_ref[...], sh_ref[...], sc_ref[...])
    logits = lax.dot_general(rw_ref[...], h_ref[...], (((1,), (1,)), ((), ())), precision=lax.Precision.HIGHEST,
                             preferred_element_type=f32)
    e = jnp.exp(logits - jnp.max(logits, axis=0, keepdims=True))
    aff_ref[...] = e / jnp.sum(e, axis=0, keepdims=True)


def moe_router(x, g, shift, scale, router_wt, *, tm=512):
    m, d = x.shape
    ne = router_wt.shape[0]
    tm = _row_tile(m, tm)
    vec = pl.BlockSpec((1, d), lambda i: (0, 0))
    return pl.pallas_call(
        _router_kernel,
        grid=(m // tm,),
        in_specs=[pl.BlockSpec((tm, d), lambda i: (i, 0)), vec, vec, vec,
                  pl.BlockSpec((ne, d), lambda i: (0, 0))],
        out_specs=[pl.BlockSpec((tm, d), lambda i: (i, 0)),
                   pl.BlockSpec((ne, tm), lambda i: (0, i))],
        out_shape=[jax.ShapeDtypeStruct((m, d), f32),
                   jax.ShapeDtypeStruct((ne, m), f32)],
        compiler_params=_cparams("parallel"),
        name="moe_router",
    )(x, g, shift, scale, router_wt)


def _select_kernel(aff_ref, idx_ref, gate_ref, off_ref, num_ref, cnt_s, *, cap, jb):
    nb = aff_ref.shape[0]
    hi = lax.Precision.HIGHEST
    aff = aff_ref[...]
    keys = pltpu.bitcast(aff, i32)

    def total(m):
        return jnp.sum(jnp.sum(m.astype(i32), axis=0, keepdims=True), axis=1, keepdims=True)

    def bs(i, thr):
        cand = thr | (jnp.int32(1) << (30 - i))
        return jnp.where(total(keys >= cand) >= cap, cand, thr)

    thr = lax.fori_loop(0, 31, bs, jnp.zeros((1, 1), i32))
    gt = keys > thr
    eq = keys == thr
    need = (cap - total(gt)).astype(f32)

    tri = (lax.broadcasted_iota(i32, (LANES, LANES), 0) <= lax.broadcasted_iota(i32, (LANES, LANES), 1)).astype(bf16)
    bi = lax.broadcasted_iota(i32, (nb, nb), 0)
    bj = lax.broadcasted_iota(i32, (nb, nb), 1)

    def block_counts(m):
        within = jnp.dot(m.astype(f32).astype(bf16), tri, preferred_element_type=f32)
        rows = jnp.broadcast_to(within[:, LANES - 1:LANES], (nb, LANES))
        before = jnp.dot((bj < bi).astype(f32), rows, precision=hi, preferred_element_type=f32)
        return within, rows, before

    w_eq, _, b_eq = block_counts(eq)
    sel = gt | (eq & (b_eq + w_eq - eq.astype(f32) < need))
    w_sel, rows, before = block_counts(sel)
    cnt_s[...] = jnp.where(sel, before + w_sel, 0.0)
    off_ref[...] = before[:, 0:1].astype(i32)
    num_ref[...] = rows[:, 0:1].astype(i32)

    sel_b = sel.astype(f32).astype(bf16)
    per_block = lax.dot_general(jnp.ones((SUBLANES, LANES), bf16), sel_b, (((1,), (1,)), ((), ())),
                                preferred_element_type=f32)
    cum_end = jnp.dot(per_block, (bi <= bj).astype(f32), precision=hi, preferred_element_type=f32)[0:1, :]
    lane_b = lax.broadcasted_iota(i32, (jb, nb), 1).astype(f32)
    lane_t = lax.broadcasted_iota(i32, (jb, LANES), 1).astype(f32)
    row = lax.broadcasted_iota(i32, (jb, 1), 0).astype(f32)

    def group(r, carry):
        j0 = r * jb
        slot = row + lax.convert_element_type(j0, f32)
        blk = jnp.sum((cum_end <= slot).astype(f32), axis=1, keepdims=True)
        onehot = (lane_b == blk).astype(f32)
        cnt_rows = jnp.dot(onehot, cnt_s[...], precision=hi, preferred_element_type=f32)
        aff_rows = jnp.dot(onehot, aff_ref[...], precision=hi, preferred_element_type=f32)
        hit = cnt_rows == slot + 1.0
        tok = blk * LANES + jnp.sum(jnp.where(hit, lane_t, 0.0), axis=1, keepdims=True)
        idx_ref[pl.ds(j0, jb), :] = tok.astype(i32)
        gate_ref[pl.ds(j0, jb), :] = jnp.sum(jnp.where(hit, aff_rows, 0.0), axis=1, keepdims=True)
        return carry

    lax.fori_loop(0, cap // jb, group, 0)


def moe_select(aff_t, cap):
    ne, T = aff_t.shape
    jb = min(cap, LANES)
    tile = LANES * LANES
    tp = -(-T // tile) * tile
    aff3 = jnp.pad(aff_t, ((0, 0), (0, tp - T))).reshape(ne, tp // LANES, LANES)
    nb = tp // LANES
    slots = pl.BlockSpec((None, cap, 1), lambda e: (e, 0, 0))
    blocks = pl.BlockSpec((None, nb, 1), lambda e: (e, 0, 0))
    return pl.pallas_call(
        functools.partial(_select_kernel, cap=cap, jb=jb),
        grid=(ne,),
        in_specs=[pl.BlockSpec((None, nb, LANES), lambda e: (e, 0, 0))],
        out_specs=[slots, slots, blocks, blocks],
        out_shape=[jax.ShapeDtypeStruct((ne, cap, 1), i32), jax.ShapeDtypeStruct((ne, cap, 1), f32),
                   jax.ShapeDtypeStruct((ne, nb, 1), i32), jax.ShapeDtypeStruct((ne, nb, 1), i32)],
        scratch_shapes=[pltpu.VMEM((nb, LANES), f32)],
        compiler_params=_cparams("parallel"),
        name="moe_select",
    )(aff3)


def _expert_kernel(idx_ref, gate_ref, h_hbm, wgu_ref, wd_ref, ye_ref, xbuf, sem, *, rows, nsteps):
    step = pl.program_id(0) * pl.num_programs(1) + pl.program_id(1)
    slot = step % 2
    d = wgu_ref.shape[0]
    nq = d // LANES

    def row_copy(stp, slt, r):
        t = idx_ref[stp * rows + r]
        return pltpu.make_async_copy(h_hbm.at[pl.ds(t, 1)], xbuf.at[slt, pl.ds(r, 1)], sem.at[slt])

    def block_wait(slt):
        pltpu.make_async_copy(h_hbm.at[pl.ds(0, rows)], xbuf.at[slt], sem.at[slt]).wait()

    @pl.when(step == 0)
    def _():
        def body(r, c):
            row_copy(step, slot, r).start()
            return c

        lax.fori_loop(0, rows, body, 0, unroll=8)

    nxt = jnp.minimum(step + 1, nsteps - 1)
    for r in range(rows):
        row_copy(nxt, 1 - slot, r).start()

    block_wait(slot)
    x = xbuf[slot].astype(bf16)
    gu = jnp.dot(x, wgu_ref[...], preferred_element_type=f32)
    ff = gu.shape[1] // 2
    hid = (_silu(gu[:, :ff]) * gu[:, ff:]).astype(bf16)
    gate = gate_ref[...]
    nw = 4 * LANES
    for c in range(d // nw):
        y = jnp.dot(hid, wd_ref[:, c * nw:(c + 1) * nw], preferred_element_type=f32) * gate
        for i in range(nw // LANES):
            ye_ref[pl.ds(c * (nw // LANES) + i, rows, stride=nq), :] = y[:, i * LANES:(i + 1) * LANES]

    @pl.when(step == nsteps - 1)
    def _():
        block_wait(1 - slot)


def moe_experts(idx_flat, gate, h2, w_gu, w_down, *, rows=256):
    ne, cap, _ = gate.shape
    d, ff = w_down.shape[2], w_down.shape[1]
    nq = d // LANES
    rows = _row_tile(cap, rows)
    nblk = cap // rows
    return pl.pallas_call(
        functools.partial(_expert_kernel, rows=rows, nsteps=ne * nblk),
        grid_spec=pltpu.PrefetchScalarGridSpec(
            num_scalar_prefetch=1,
            grid=(ne, nblk),
            in_specs=[pl.BlockSpec((None, rows, 1), lambda e, j, idx: (e, j, 0)),
                      pl.BlockSpec(memory_space=pl.ANY),
                      pl.BlockSpec((None, d, 2 * ff), lambda e, j, idx: (e, 0, 0), pipeline_mode=pl.Buffered(1)),
                      pl.BlockSpec((None, ff, d), lambda e, j, idx: (e, 0, 0), pipeline_mode=pl.Buffered(1))],
            out_specs=pl.BlockSpec((rows * nq, LANES), lambda e, j, idx: (e * nblk + j, 0)),
            scratch_shapes=[pltpu.VMEM((2, rows, d), f32), pltpu.SemaphoreType.DMA((2,))]),
        out_shape=jax.ShapeDtypeStruct((ne * cap * nq, LANES), f32),
        compiler_params=_cparams("arbitrary", "arbitrary"),
        name="moe_experts",
    )(idx_flat, gate, h2, w_gu, w_down)


COMBINE_WINDOW_LOG2 = 5
COMBINE_WINDOW = 1 << COMBINE_WINDOW_LOG2


def _combine_kernel(idx_ref, off_ref, num_ref, ye_hbm, x_ref, g_ref, gate_ref, o_ref, stage, extra, acc, sem,
                    *, cap, ne, nbk):
    b = pl.program_id(0)
    slot = b % 2
    tb, d = x_ref.shape
    nq = d // LANES
    win = COMBINE_WINDOW

    def window_start(off):
        return jnp.minimum(off, cap - win)

    def start_windows(bb, slt):
        for e in range(ne):
            s0 = window_start(off_ref[e * nbk + bb])
            pltpu.make_async_copy(ye_hbm.at[_slab(e * cap + s0, win, nq)], stage.at[slt, _slab(e * win, win, nq)],
                                  sem.at[slt]).start()

    @pl.when(b == 0)
    def _():
        start_windows(b, slot)

    @pl.when(b + 1 < nbk)
    def _():
        start_windows(b + 1, 1 - slot)

    pltpu.make_async_copy(ye_hbm.at[_slab(0, ne * win, nq)], stage.at[slot], sem.at[slot]).wait()

    acc[...] = jnp.zeros_like(acc)
    for e in range(ne):
        off = off_ref[e * nbk + b]
        n = num_ref[e * nbk + b]
        s0 = window_start(off)
        n_win = jnp.minimum(n, s0 + win - off)

        def add_row(j, carry, e=e, s0=s0):
            t = idx_ref[e * cap + j] - b * tb
            acc[_slab(t, 1, nq), :] = acc[_slab(t, 1, nq), :] + stage[slot, _slab(e * win + (j - s0), 1, nq), :]
            return carry

        lax.fori_loop(off, off + n_win, add_row, 0)

        def overflow(c, carry, e=e, off=off, n=n, n_win=n_win):
            j0 = off + n_win + c * win
            s1 = window_start(j0)
            cp = pltpu.make_async_copy(ye_hbm.at[_slab(e * cap + s1, win, nq)], extra, sem.at[2])
            cp.start()
            cp.wait()

            def add_extra(j, carry2):
                t = idx_ref[e * cap + j] - b * tb
                acc[_slab(t, 1, nq), :] = acc[_slab(t, 1, nq), :] + extra[_slab(j - s1, 1, nq), :]
                return carry2

            lax.fori_loop(j0, jnp.minimum(j0 + win, off + n), add_extra, 0)
            return carry

        lax.fori_loop(0, lax.shift_right_logical(n - n_win + win - 1, COMBINE_WINDOW_LOG2), overflow, 0)

    sq = jnp.zeros((tb, LANES), f32)
    for q in range(nq):
        part = acc[pl.ds(q, tb, stride=nq), :]
        sq = sq + part * part
    rinv = lax.rsqrt(jnp.sum(sq, axis=-1, keepdims=True) * (1.0 / d) + EPS)
    for q in range(nq):
        sl = slice(q * LANES, (q + 1) * LANES)
        part = acc[pl.ds(q, tb, stride=nq), :]
        o_ref[:, sl] = x_ref[:, sl] + gate_ref[:, sl] * ((part * rinv) * g_ref[:, sl])


def moe_combine(idx_flat, off, num, ye, x, g, gate, *, cap):
    T, d = x.shape
    nq = d // LANES
    tb = LANES
    nbk = T // tb
    ne = off.shape[0] // nbk
    win = COMBINE_WINDOW
    assert cap >= win
    vec = pl.BlockSpec((1, d), lambda i, *_: (0, 0))
    return pl.pallas_call(
        functools.partial(_combine_kernel, cap=cap, ne=ne, nbk=nbk),
        grid_spec=pltpu.PrefetchScalarGridSpec(
            num_scalar_prefetch=3,
            grid=(nbk,),
            in_specs=[pl.BlockSpec(memory_space=pl.ANY),
                      pl.BlockSpec((tb, d), lambda i, *_: (i, 0)), vec, vec],
            out_specs=pl.BlockSpec((tb, d), lambda i, *_: (i, 0)),
            scratch_shapes=[pltpu.VMEM((2, ne * win * nq, LANES), f32), pltpu.VMEM((win * nq, LANES), f32),
                            pltpu.VMEM((tb * nq, LANES), f32), pltpu.SemaphoreType.DMA((3,))]),
        out_shape=jax.ShapeDtypeStruct((T, d), f32),
        compiler_params=_cparams("arbitrary"),
        name="moe_combine",
    )(idx_flat, off, num, ye, x, g, gate)


def moe_block(x, g_in, shift, scale, g_out, gate, p):
    T = x.shape[0]
    cap = EC_CAPACITY * T // N_EXPERTS
    nbk = T // LANES
    h3, aff_t = moe_router(x, g_in, shift, scale, p["router_wt"])
    idx, gates, off, num = moe_select(aff_t, cap)
    idx_flat = idx.reshape(-1)
    ye = moe_experts(idx_flat, gates, h3, p["w_gu"], p["w_down"], rows=min(cap, 512))
    return moe_combine(idx_flat, off[:, :nbk, 0].reshape(-1), num[:, :nbk, 0].reshape(-1), ye, x, g_out, gate, cap=cap)


def _conv_silu_kernel(x_ref, xp_ref, xn_ref, cw_ref, cb_ref, o_ref, ext_s, *, nt, tt):
    ti = pl.program_id(0)
    ext_s[0:8, :] = jnp.where(ti == 0, 0.0, xp_ref[...])
    ext_s[8:8 + tt, :] = x_ref[...]
    ext_s[8 + tt:16 + tt, :] = jnp.where(ti == nt - 1, 0.0, xn_ref[...])
    acc = cb_ref[...]
    for k in range(CONV_W):
        acc = acc + ext_s[pl.ds(6 + k, tt), :] * cw_ref[k:k + 1, :]
    o_ref[...] = _silu(acc)


def conv_silu(zx, conv_w, conv_b, col0, *, tt=512, tc=1024):
    T = zx.shape[0]
    n = conv_w.shape[1]
    tt = _row_tile(T, tt)
    nt = T // tt
    tb = tt // 8
    cb0 = col0 // tc
    return pl.pallas_call(
        functools.partial(_conv_silu_kernel, nt=nt, tt=tt),
        grid=(nt, n // tc),
        in_specs=[pl.BlockSpec((tt, tc), lambda t, j: (t, cb0 + j)),
                  pl.BlockSpec((8, tc), lambda t, j: (jnp.maximum(t * tb - 1, 0), cb0 + j)),
                  pl.BlockSpec((8, tc), lambda t, j: (jnp.minimum((t + 1) * tb, T // 8 - 1), cb0 + j)),
                  pl.BlockSpec((CONV_W, tc), lambda t, j: (0, j)),
                  pl.BlockSpec((1, tc), lambda t, j: (0, j))],
        out_specs=pl.BlockSpec((tt, tc), lambda t, j: (t, j)),
        out_shape=jax.ShapeDtypeStruct((T, n), f32),
        scratch_shapes=[pltpu.VMEM((tt + 16, tc), f32)],
        compiler_params=_cparams("parallel", "parallel"),
        name="conv_silu",
    )(zx, zx, zx, conv_w, conv_b)


def _ssd_prep_kernel(raw_ref, bias_ref, alog_ref, dt_ref, ac_ref, act_ref):
    L = SSD_CHUNK
    nh = raw_ref.shape[1] // 2
    hg = nh // SSD_GROUPS
    dt = _softplus(raw_ref[...] + bias_ref[...])
    dta = dt * (-jnp.exp(alog_ref[...]))
    r = lax.broadcasted_iota(i32, (L, L), 0)
    c = lax.broadcasted_iota(i32, (L, L), 1)
    hi = lax.Precision.HIGHEST
    acs = [jnp.dot((c <= r).astype(f32), dta[:, :nh], precision=hi, preferred_element_type=f32),
           jnp.dot((c >= r).astype(f32), dta[:, nh:], precision=hi, preferred_element_type=f32)]
    for d in range(2):
        act = acs[d].T
        for g in range(SSD_GROUPS):
            dt_ref[d, g] = dt[:, d * nh + g * hg:d * nh + (g + 1) * hg]
            ac_ref[d, g] = acs[d][:, g * hg:(g + 1) * hg]
            act_ref[d, g] = act[g * hg:(g + 1) * hg, :]


def ssd_prep(dt_raw, dt_bias, a_log):
    T, nh2 = dt_raw.shape
    nh = nh2 // 2
    hg = nh // SSD_GROUPS
    nc = T // SSD_CHUNK
    vec = pl.BlockSpec((1, nh2), lambda c: (0, 0))
    return pl.pallas_call(
        _ssd_prep_kernel,
        grid=(nc,),
        in_specs=[pl.BlockSpec((SSD_CHUNK, nh2), lambda c: (c, 0)), vec, vec],
        out_specs=[pl.BlockSpec((2, SSD_GROUPS, SSD_CHUNK, hg), lambda c: (0, 0, c, 0)),
                   pl.BlockSpec((2, SSD_GROUPS, SSD_CHUNK, hg), lambda c: (0, 0, c, 0)),
                   pl.BlockSpec((2, SSD_GROUPS, None, hg, SSD_CHUNK), lambda c: (0, 0, c, 0, 0))],
        out_shape=[jax.ShapeDtypeStruct((2, SSD_GROUPS, T, hg), f32),
                   jax.ShapeDtypeStruct((2, SSD_GROUPS, T, hg), f32),
                   jax.ShapeDtypeStruct((2, SSD_GROUPS, nc, hg, SSD_CHUNK), f32)],
        compiler_params=_cparams("parallel"),
        name="ssd_prep",
    )(dt_raw, dt_bias, a_log)


def _ssd_scan_kernel(*refs, reverse, final, hg):
    if final:
        (xs_ref, b_ref, c_ref, dt_ref, ac_ref, act_ref, s0_ref, yf_ref, z_ref, dsk_ref, ng_ref,
         out_ref, st_ref, yz_s) = refs
    else:
        (xs_ref, b_ref, c_ref, dt_ref, ac_ref, act_ref, s0_ref, out_ref, st_ref) = refs
    L = SSD_CHUNK
    P = SSD_HEADDIM
    ci = pl.program_id(0)
    g = pl.program_id(1)

    @pl.when(ci == 0)
    def _():
        st_ref[g] = s0_ref[g]

    xs = xs_ref[...]
    dt16 = dt_ref[...]
    ac16 = ac_ref[...]
    act = act_ref[...]
    expand = (lax.broadcasted_iota(i32, (hg, hg * P), 0)
              == lax.broadcasted_iota(i32, (hg, hg * P), 1) // P).astype(f32)
    dtx = jnp.dot(dt16, expand, precision=lax.Precision.HIGHEST, preferred_element_type=f32)
    xq = (xs * dtx).astype(bf16)
    bm = b_ref[...]
    cm = c_ref[...]
    cb = lax.dot_general(cm.astype(bf16), bm.astype(bf16), (((1,), (1,)), ((), ())), preferred_element_type=f32)
    bt = bm.T
    li = lax.broadcasted_iota(i32, (L, L), 0)
    si = lax.broadcasted_iota(i32, (L, L), 1)
    mask = (li <= si) if reverse else (li >= si)
    last = 0 if reverse else L - 1
    st = st_ref[g]
    stb = st.astype(bf16)
    ys = []
    new_st = []
    for hh in range(hg):
        hs = slice(hh * P, (hh + 1) * P)
        col = ac16[:, hh:hh + 1]
        row = act[hh:hh + 1, :]
        gm = (jnp.exp(jnp.where(mask, col - row, -jnp.inf)) * cb).astype(bf16)
        ce = (cm * jnp.exp(col)).astype(bf16)
        xqh = xq[:, hs]
        ys.append(jnp.dot(jnp.concatenate([gm, ce], axis=1), jnp.concatenate([xqh, stb[:, hs]], axis=0),
                          preferred_element_type=f32))
        tot = row[:, last:last + 1]
        bdt = (bt * jnp.exp(tot - row)).astype(bf16)
        new_st.append(st[:, hs] * jnp.exp(tot) + jnp.dot(bdt, xqh, preferred_element_type=f32))
    y = jnp.concatenate(ys, axis=1)
    st_ref[g] = jnp.concatenate(new_st, axis=1)
    if not final:
        out_ref[...] = y
        return
    yz_s[g] = (yf_ref[...] + y + dsk_ref[...] * xs) * _silu(z_ref[...])

    @pl.when(g == SSD_GROUPS - 1)
    def _():
        w = hg * P
        ss = None
        for k in range(SSD_GROUPS):
            v = yz_s[k]
            s = jnp.sum(v * v, axis=-1, keepdims=True)
            ss = s if ss is None else ss + s
        rinv = lax.rsqrt(ss * (1.0 / (w * SSD_GROUPS)) + EPS)
        for k in range(SSD_GROUPS):
            out_ref[:, k * w:(k + 1) * w] = ((yz_s[k] * rinv) * ng_ref[:, k * w:(k + 1) * w]).astype(out_ref.dtype)


def ssd_scan(xbc, dtg, acg, act, s0, fin=None, *, reverse):
    T = xbc.shape[0]
    L = SSD_CHUNK
    nc = T // L
    hg = dtg.shape[-1]
    w = hg * SSD_HEADDIM
    d_inner = w * SSD_GROUPS
    nb0 = d_inner // SSD_STATE
    final = fin is not None
    cix = (lambda c: nc - 1 - c) if reverse else (lambda c: c)
    full_state = pl.BlockSpec(s0.shape, lambda c, g: (0, 0, 0))
    in_specs = [
        pl.BlockSpec((L, w), lambda c, g: (cix(c), g)),
        pl.BlockSpec((L, SSD_STATE), lambda c, g: (cix(c), nb0 + g)),
        pl.BlockSpec((L, SSD_STATE), lambda c, g: (cix(c), nb0 + SSD_GROUPS + g)),
        pl.BlockSpec((None, L, hg), lambda c, g: (g, cix(c), 0)),
        pl.BlockSpec((None, L, hg), lambda c, g: (g, cix(c), 0)),
        pl.BlockSpec((None, None, hg, L), lambda c, g: (g, cix(c), 0, 0)),
        full_state,
    ]
    args = [xbc, xbc, xbc, dtg, acg, act, s0]
    scratch = []
    if final:
        yf, zx, dsk, ng = fin
        in_specs += [pl.BlockSpec((L, w), lambda c, g: (cix(c), g)),
                     pl.BlockSpec((L, w), lambda c, g: (cix(c), g)),
                     pl.BlockSpec((1, w), lambda c, g: (0, g)),
                     pl.BlockSpec((1, d_inner), lambda c, g: (0, 0))]
        args += [yf, zx, dsk, ng]
        out_spec = pl.BlockSpec((L, d_inner), lambda c, g: (cix(c), 0))
        out_shape = jax.ShapeDtypeStruct((T, d_inner), bf16)
        scratch = [pltpu.VMEM((SSD_GROUPS, L, w), f32)]
    else:
        out_spec = pl.BlockSpec((L, w), lambda c, g: (cix(c), g))
        out_shape = jax.ShapeDtypeStruct((T, d_inner), f32)
    return pl.pallas_call(
        functools.partial(_ssd_scan_kernel, reverse=reverse, final=final, hg=hg),
        grid=(nc, SSD_GROUPS),
        in_specs=in_specs,
        out_specs=[out_spec, full_state],
        out_shape=[out_shape, jax.ShapeDtypeStruct(s0.shape, f32)],
        scratch_shapes=scratch,
        compiler_params=_cparams("arbitrary", "arbitrary"),
        name="ssd_scan_bwd" if reverse else "ssd_scan_fwd",
    )(*args)


def ssd_mixer(h_args_l, h_args_c, p):
    d_inner = p["norm_g"].shape[1]
    hg = d_inner // SSD_HEADDIM // SSD_GROUPS
    s_zero = jnp.zeros((SSD_GROUPS, SSD_STATE, hg * SSD_HEADDIM), f32)

    def prep(h_args):
        zx = norm_mod_matmul(*h_args, p["w_zx"])
        dt_raw = norm_mod_matmul(*h_args, p["w_dt"], tn=p["w_dt"].shape[1])
        xbc = conv_silu(zx, p["conv_w"], p["conv_b"], d_inner)
        return (zx, xbc) + tuple(ssd_prep(dt_raw, p["dt_bias"], p["a_log"]))

    zx_c, xbc_c, dt_c, ac_c, act_c = prep(h_args_c)
    zx_l, xbc_l, dt_l, ac_l, act_l = prep(h_args_l)
    yf_c, s_c = ssd_scan(xbc_c, dt_c[0], ac_c[0], act_c[0], s_zero, reverse=False)
    yf_l, _ = ssd_scan(xbc_l, dt_l[0], ac_l[0], act_l[0], s_c, reverse=False)
    y_c, s_c = ssd_scan(xbc_c, dt_c[1], ac_c[1], act_c[1], s_zero, (yf_c, zx_c, p["d_skip"], p["norm_g"]), reverse=True)
    y_l, _ = ssd_scan(xbc_l, dt_l[1], ac_l[1], act_l[1], s_c, (yf_l, zx_l, p["d_skip"], p["norm_g"]), reverse=True)
    return y_l, y_c


def _rope_table_kernel(cos_ref, sin_ref, *, tt):
    half = HEAD_DIM // 2
    t = lax.broadcasted_iota(i32, (tt, HEAD_DIM), 0) + pl.program_id(0) * tt
    lane = lax.broadcasted_iota(i32, (tt, HEAD_DIM), 1)
    pos = jnp.where(lane < half, t // GRID_W, t % GRID_W).astype(f32)
    k = (lane % (half // 2)).astype(f32)
    inv_freq = jnp.exp(k * (-2.0 / half * math.log(ROPE_BASE)))
    ang = pos * inv_freq
    cos_ref[...] = jnp.cos(ang)
    sin_ref[...] = jnp.where(lane % half < half // 2, -1.0, 1.0) * jnp.sin(ang)


def rope_tables(T, *, tt=512):
    tt = _row_tile(T, tt)
    spec = pl.BlockSpec((tt, HEAD_DIM), lambda i: (i, 0))
    return pl.pallas_call(
        functools.partial(_rope_table_kernel, tt=tt),
        grid=(T // tt,),
        in_specs=[],
        out_specs=[spec, spec],
        out_shape=[jax.ShapeDtypeStruct((T, HEAD_DIM), f32)] * 2,
        compiler_params=_cparams("parallel"),
        name="rope_tables",
    )()


def _rope(x, cos, sin):
    n, w = x.shape
    reps = w // HEAD_DIM
    if reps > 1:
        cos = jnp.concatenate([cos] * reps, axis=1)
        sin = jnp.concatenate([sin] * reps, axis=1)
    q = HEAD_DIM // 4
    lane = lax.broadcasted_iota(i32, (n, w), 1)
    partner = jnp.where(lane % (2 * q) < q, pltpu.roll(x, w - q, axis=1), pltpu.roll(x, q, axis=1))
    return x * cos + partner * sin


def _attn_kernel(*refs, band):
    if band:
        (q_ref, kp_ref, ko_ref, kn_ref, vp_ref, vo_ref, vn_ref, cp_ref, co_ref, cn_ref, sp_ref, so_ref, sn_ref,
         kc_ref, vc_ref, sink_ref, bias_ref, o_ref) = refs
    else:
        q_ref, kc_ref, vc_ref, sink_ref, o_ref = refs
    nrow = q_ref.shape[0]
    q = q_ref[...]
    if band:
        q = _rope(q, co_ref[...], so_ref[...])
    qs = jnp.concatenate([q[:, j * HEAD_DIM:(j + 1) * HEAD_DIM] for j in range(ATTN_GROUP)], axis=0).astype(bf16)
    kc = kc_ref[...].astype(bf16)
    vc = vc_ref[...].astype(bf16)
    nt = (((1,), (1,)), ((), ()))
    scale = HEAD_DIM ** -0.5
    sink = sink_ref[...]
    s_c = lax.dot_general(qs, kc, nt, preferred_element_type=f32) * scale
    m = jnp.maximum(jnp.max(s_c, axis=1, keepdims=True), sink)
    if band:
        kb = jnp.concatenate([_rope(kp_ref[...], cp_ref[...], sp_ref[...]),
                              _rope(ko_ref[...], co_ref[...], so_ref[...]),
                              _rope(kn_ref[...], cn_ref[...], sn_ref[...])], axis=0).astype(bf16)
        vb = jnp.concatenate([vp_ref[...], vo_ref[...], vn_ref[...]], axis=0).astype(bf16)
        s_b = lax.dot_general(qs, kb, nt, preferred_element_type=f32) * scale \
            + jnp.concatenate([bias_ref[...]] * ATTN_GROUP, axis=0)
        m =jnp.maximum(m, jnp.max(s_b, axis=1, keepdims=True))
        p_b = jnp.exp(s_b - m)
    p_c = jnp.exp(s_c - m)
    den = jnp.sum(p_c, axis=1, keepdims=True) + jnp.exp(sink - m)
    if band:
        den = den + jnp.sum(p_b, axis=1, keepdims=True)
    rden = 1.0 / den
    o = jnp.dot((p_c * rden).astype(bf16), vc, preferred_element_type=f32)
    if band:
        o = o + jnp.dot((p_b * rden).astype(bf16), vb, preferred_element_type=f32)
    o_ref[...] = jnp.concatenate([o[j * nrow:(j + 1) * nrow, :] for j in range(ATTN_GROUP)], axis=1).astype(o_ref.dtype)


def window_attention(qkv_l, qkv_c, sink_rows, cos, sin):
    T = qkv_l.shape[0]
    Lc = qkv_c.shape[0]
    nkv = qkv_l.shape[1] // HEAD_DIM // (ATTN_GROUP + 2)
    nh = nkv * ATTN_GROUP
    k0, v0 = nh, nh + nkv
    gw = ATTN_GROUP * HEAD_DIM
    qb = ATTN_QBLOCKS
    blk = qb * WINDOW
    nq = T // blk
    nw = T // WINDOW
    rows = [(WINDOW, lambda i: jnp.maximum(qb * i - 1, 0)), (blk, lambda i: i),
            (WINDOW, lambda i: jnp.minimum(qb * (i + 1), nw - 1))]
    kv = lambda c0: [pl.BlockSpec((r, HEAD_DIM), (lambda g, i, f=f, c0=c0: (f(i), c0 + g))) for r, f in rows]
    tab = [pl.BlockSpec((r, HEAD_DIM), (lambda g, i, f=f: (f(i), 0))) for r, f in rows]
    ctx_specs = lambda: [pl.BlockSpec((Lc, HEAD_DIM), lambda g, i: (0, k0 + g)),
                         pl.BlockSpec((Lc, HEAD_DIM), lambda g, i: (0, v0 + g))]
    r = jnp.arange(blk)[:, None] + WINDOW
    c = jnp.arange(blk + 2 * WINDOW)[None, :]

    def bias_of(i):
        k_abs = c + (i * blk - WINDOW)
        return jnp.where((jnp.abs(r - c) <= WINDOW) & (k_abs >= 0) & (k_abs < T), 0.0, -jnp.inf).astype(f32)

    bias = jnp.stack([bias_of(0), bias_of(min(1, nq - 1)), bias_of(nq - 1)])
    which = lambda i: jnp.where(i == 0, 0, jnp.where(i == nq - 1, 2, 1))
    o_l = pl.pallas_call(
        functools.partial(_attn_kernel, band=True),
        grid=(nkv, nq),
        in_specs=[pl.BlockSpec((blk, gw), lambda g, i: (i, g))] + kv(k0) + kv(v0) + tab + tab + ctx_specs()
        + [pl.BlockSpec((None, ATTN_GROUP * blk, 1), lambda g, i: (g, 0, 0)),
           pl.BlockSpec((None, blk, blk + 2 * WINDOW), lambda g, i: (which(i), 0, 0))],
        out_specs=pl.BlockSpec((blk, gw), lambda g, i: (i, g)),
        out_shape=jax.ShapeDtypeStruct((T, nh * HEAD_DIM), bf16),
        compiler_params=_cparams("parallel", "parallel"),
        name="window_attention",
    )(qkv_l, *([qkv_l] * 6), cos, cos, cos, sin, sin, sin, qkv_c, qkv_c, sink_rows[0], bias)
    o_c = pl.pallas_call(
        functools.partial(_attn_kernel, band=False),
        grid=(nkv, 1),
        in_specs=[pl.BlockSpec((Lc, gw), lambda g, i: (0, g))] + ctx_specs()
        + [pl.BlockSpec((None, ATTN_GROUP * Lc, 1), lambda g, i: (g, 0, 0))],
        out_specs=pl.BlockSpec((Lc, gw), lambda g, i: (0, g)),
        out_shape=jax.ShapeDtypeStruct((Lc, nh * HEAD_DIM), bf16),
        compiler_params=_cparams("parallel", "parallel"),
        name="context_attention",
    )(qkv_c, qkv_c, qkv_c, sink_rows[1])
    return o_l, o_c


def _adaln_kernel(c_ref, dn_ref, up_ref, b_ref, o_ref):
    hi = lax.Precision.HIGHEST
    t = jnp.dot(jax.nn.silu(c_ref[...]), dn_ref[...], precision=hi, preferred_element_type=f32)
    o_ref[...] = jnp.dot(t, up_ref[...], precision=hi, preferred_element_type=f32) + b_ref[...]


def adaln(cond, w_down, w_up, b, *, tn=2048):
    depth, d, r = w_down.shape
    n = w_up.shape[2]
    rows = cond.shape[0]
    return pl.pallas_call(
        _adaln_kernel,
        grid=(depth, n // tn),
        in_specs=[pl.BlockSpec((rows, d), lambda l, j: (0, 0)),
                  pl.BlockSpec((None, d, r), lambda l, j: (l, 0, 0)),
                  pl.BlockSpec((None, r, tn), lambda l, j: (l, 0, j)),
                  pl.BlockSpec((None, 1, tn), lambda l, j: (l, 0, j))],
        out_specs=pl.BlockSpec((None, rows, tn), lambda l, j: (l, 0, j)),
        out_shape=jax.ShapeDtypeStruct((depth, rows, n), f32),
        compiler_params=_cparams("parallel", "parallel"),
        name="adaln",
    )(cond, w_down, w_up, b.reshape(depth, 1, n))


def kernel(x, c, ctx, c_ctx, ada_down, ada_up, ada_b, norm_g, router_w, moe_w_gate, moe_w_up, moe_w_down,
           rg_w_in, rg_conv_w, rg_conv_b, rg_gate_w, rg_gate_b, rg_lambda, rg_w_out,
           ssd_w_in, ssd_conv_w, ssd_conv_b, ssd_dt_bias, ssd_a_log, ssd_d, ssd_norm_g, ssd_w_out,
           attn_w_qkv, attn_sink, attn_w_out):
    assert x.shape[0] == 1 and ctx.shape[0] == 1 and c.shape[0] == 1
    depth = ada_down.shape[0]
    T, d = x.shape[1], x.shape[2]
    Lc = ctx.shape[1]
    xl, xc = x[0], ctx[0]
    cond = jnp.concatenate([c, c_ctx[None, :], jnp.zeros((SUBLANES - 2, d), f32)], axis=0)
    mods = adaln(cond, ada_down, ada_up, ada_b)
    row = lambda v: v.reshape(1, -1)

    for i in range(depth):
        kind, slot = i % N_MIXERS, i // N_MIXERS
        ml = [mods[i, 0:1, k * d:(k + 1) * d] for k in range(N_MOD)]
        mc = [mods[i, 1:2, k * d:(k + 1) * d] for k in range(N_MOD)]
        g = [row(norm_g[i, k]) for k in range(4)]
        last = i == depth - 1
        in_l = (xl, g[0], ml[0], ml[1])
        in_c = (xc, g[0], mc[0], mc[1])
        if kind == 0:
            p = dict(w_in=to_bf16(rg_w_in[slot]), conv_w=rg_conv_w[slot], conv_b=row(rg_conv_b[slot]),
                     gate_w=to_bf16(rg_gate_w[slot]), gate_b=rg_gate_b[slot],
                     lam=rg_lambda[slot].reshape(2, 1, -1))
            y_l, y_c = rglru_mixer(in_l, in_c, p)
            w_out = to_bf16(rg_w_out[slot])
        elif kind == 1:
            d_inner = ssd_norm_g.shape[1]
            w_in = ssd_w_in[slot]
            n_zx = d_inner + ssd_conv_w.shape[2]
            p = dict(w_zx=to_bf16(w_in, col0=0, ncols=n_zx), w_dt=to_bf16(w_in, col0=n_zx, ncols=w_in.shape[1] - n_zx),
                     conv_w=ssd_conv_w[slot], conv_b=row(ssd_conv_b[slot]),
                     dt_bias=row(ssd_dt_bias[slot]), a_log=row(ssd_a_log[slot]),
                     d_skip=row(jnp.repeat(ssd_d[slot], SSD_HEADDIM)), norm_g=row(ssd_norm_g[slot]))
            y_l, y_c = ssd_mixer(in_l, in_c, p)
            w_out = to_bf16(ssd_w_out[slot])
        else:
            w_qkv = to_bf16(attn_w_qkv[slot])
            qkv_l = norm_mod_matmul(*in_l, w_qkv)
            qkv_c = norm_mod_matmul(*in_c, w_qkv)
            sk = attn_sink[slot].reshape(-1, ATTN_GROUP, 1, 1)
            sink_rows = [jnp.broadcast_to(sk, sk.shape[:2] + (n, 1)).reshape(sk.shape[0], ATTN_GROUP * n, 1)
                         for n in (ATTN_QBLOCKS * WINDOW, Lc)]
            cos, sin = rope_tables(T)
            y_l, y_c = window_attention(qkv_l, qkv_c, sink_rows, cos, sin)
            w_out = to_bf16(attn_w_out[slot])
        mp = dict(router_wt=router_w[i].T, w_gu=to_bf16(moe_w_gate[i], moe_w_up[i]), w_down=to_bf16(moe_w_down[i]))
        xl = matmul_norm_res(y_l, w_out, g[1], ml[2], xl)
        xl = moe_block(xl, g[2], ml[3], ml[4], g[3], ml[5], mp)
        if not last:
            xc = matmul_norm_res(y_c, w_out, g[1], mc[2], xc)
            xc = moe_block(xc, g[2], mc[3], mc[4], g[3], mc[5], mp)
    return xl[None]
```

```python
import functools
import math

import jax
import jax.numpy as jnp
from jax import lax
from jax.experimental import pallas as pl
from jax.experimental.pallas import tpu as pltpu

f32 = jnp.float32
bf16 = jnp.bfloat16
i32 = jnp.int32

EPS = 1e-6
N_MIXERS = 3
N_MOD = 6
CONV_W = 4
RG_HEADS = 16
RG_C = 8.0
SSD_HEADDIM = 64
SSD_STATE = 128
SSD_GROUPS = 8
SSD_CHUNK = 128
HEAD_DIM = 128
ATTN_GROUP = 4
WINDOW = 128
ATTN_QBLOCKS = 2
ROPE_BASE = 10000.0
GRID_W = 64
N_EXPERTS = 16
EC_CAPACITY = 2

LANES = 128
SUBLANES = 8
VMEM_LIMIT = 56 * 1024 * 1024


def _cparams(*sem):
    return pltpu.CompilerParams(dimension_semantics=sem, vmem_limit_bytes=VMEM_LIMIT)


def _row_tile(m, want):
    t = min(m, want)
    assert m % t == 0
    return t


def _sigmoid(x):
    return 0.5 * jnp.tanh(0.5 * x) + 0.5


def _silu(x):
    return x * _sigmoid(x)


def _rms_mod(x, g, shift, scale):
    y = x * lax.rsqrt(jnp.mean(x * x, axis=-1, keepdims=True) + EPS)
    return (y * g) * (1.0 + scale) + shift


CAST_BLOCK_BYTES = 4 * 1024 * 1024


def _cast_kernel(*refs):
    *ins, o_ref = refs
    col = 0
    for r in ins:
        w = r.shape[-1]
        o_ref[:, col:col + w] = r[...].astype(o_ref.dtype)
        col += w


def to_bf16(*ws, layer, col0=0, ncols=None):
    lead = ws[0].shape[1:-1]
    xs = [w.reshape(-1, w.shape[-1]) for w in ws]
    m = xs[0].shape[0] // ws[0].shape[0]
    if len(xs) > 1 or ncols is None:
        assert col0 == 0 and ncols is None
        tcs = [x.shape[1] for x in xs]
        n = sum(tcs)
        nj = 1
    else:
        n = ncols
        tc = 2048 if (n % 2048 == 0 and col0 % 2048 == 0) else n
        assert col0 % tc == 0
        tcs = [tc]
        nj = n // tc
    if nj == 1 and len(xs) == 1 and n % 2048 == 0 and n > 2048:
        tcs, nj = [2048], n // 2048
    want = max(SUBLANES, CAST_BLOCK_BYTES // (4 * sum(tcs)))
    tr = _row_tile(m, 1 << (want.bit_length() - 1))
    c0 = col0 // tcs[0]
    r0 = layer * (m // tr)
    out = pl.pallas_call(
        _cast_kernel,
        grid=(m // tr, nj),
        in_specs=[pl.BlockSpec((tr, tc), lambda i, j: (i + r0, j + c0)) for tc in tcs],
        out_specs=pl.BlockSpec((tr, sum(tcs)), lambda i, j: (i, j)),
        out_shape=jax.ShapeDtypeStruct((m, n), bf16),
        compiler_params=_cparams("parallel", "parallel"),
        name="to_bf16",
    )(*xs)
    return out.reshape(lead + (n,))


def _nmm_kernel(x_ref, g_ref, sh_ref, sc_ref, w_ref, o_ref, a_ref):
    @pl.when(pl.program_id(1) == 0)
    def _():
        a_ref[...] = _rms_mod(x_ref[...], g_ref[...], sh_ref[...], sc_ref[...]).astype(a_ref.dtype)

    o_ref[...] = jnp.dot(a_ref[...], w_ref[...], preferred_element_type=f32).astype(o_ref.dtype)


def norm_mod_matmul(x, g, shift, scale, w, *, tm=512, tn=1024, out_dtype=f32):
    m, d = x.shape
    n = w.shape[1]
    tm = _row_tile(m, tm)
    tn = _row_tile(n, tn)
    vec = pl.BlockSpec((1, d), lambda i, j: (0, 0))
    return pl.pallas_call(
        _nmm_kernel,
        grid=(m // tm, n // tn),
        in_specs=[pl.BlockSpec((tm, d), lambda i, j: (i, 0)), vec, vec, vec,
                  pl.BlockSpec((d, tn), lambda i, j: (0, j))],
        out_specs=pl.BlockSpec((tm, tn), lambda i, j: (i, j)),
        out_shape=jax.ShapeDtypeStruct((m, n), out_dtype),
        scratch_shapes=[pltpu.VMEM((tm, d), bf16)],
        compiler_params=_cparams("parallel", "arbitrary"),
        name="norm_mod_matmul",
    )(x, g, shift, scale, w)


def _mnr_kernel(a_ref, w_ref, g_ref, gate_ref, r_ref, o_ref, y_ref, ss_ref, *, nj, n_total):
    j = pl.program_id(1)

    @pl.when(j == 0)
    def _():
        ss_ref[...] = jnp.zeros_like(ss_ref)

    @pl.when(j < nj)
    def _():
        y = jnp.dot(a_ref[...], w_ref[...], preferred_element_type=f32)
        y_ref[j] = y
        ss_ref[...] += jnp.sum(y * y, axis=-1, keepdims=True)

    @pl.when(j >= nj)
    def _():
        rinv = lax.rsqrt(ss_ref[...] * (1.0 / n_total) + EPS)
        o_ref[...] = r_ref[...] + gate_ref[...] * ((y_ref[j - nj] * rinv) * g_ref[...])


def matmul_norm_res(a, w, g, gate, resid, *, tn=512):
    m, k = a.shape
    n = w.shape[1]
    tm = _row_tile(m, 4 * 1024 * 1024 // k)
    tn = _row_tile(n, tn)
    nj = n // tn
    ph2 = lambda j: jnp.maximum(j - nj, 0)
    return pl.pallas_call(
        functools.partial(_mnr_kernel, nj=nj, n_total=n),
        grid=(m // tm, 2 * nj),
        in_specs=[pl.BlockSpec((tm, k), lambda i, j: (i, 0)),
                  pl.BlockSpec((k, tn), lambda i, j: (0, jnp.minimum(j, nj - 1))),
                  pl.BlockSpec((1, tn), lambda i, j: (0, ph2(j))),
                  pl.BlockSpec((1, tn), lambda i, j: (0, ph2(j))),
                  pl.BlockSpec((tm, tn), lambda i, j: (i, ph2(j)))],
        out_specs=pl.BlockSpec((tm, tn), lambda i, j: (i, ph2(j))),
        out_shape=jax.ShapeDtypeStruct((m, n), f32),
        scratch_shapes=[pltpu.VMEM((nj, tm, tn), f32), pltpu.VMEM((tm, 1), f32)],
        compiler_params=_cparams("parallel", "arbitrary"),
        name="matmul_norm_res",
    )(a, w, g, gate, resid)


def _softplus(x):
    return jnp.maximum(x, 0.0) + jnp.log1p(jnp.exp(-jnp.abs(x)))


def _rg_sweep_kernel(*refs, reverse, final, nt, tt):
    if final:
        (v_ref, vp_ref, vn_ref, cw_ref, cb_ref, gw_ref, gb_ref, lam_ref, h0_ref, hf_ref, g_ref,
         out_ref, hT_ref, ext_s, a_s, u_s, hs_s, carry_s) = refs
    else:
        (v_ref, vp_ref, vn_ref, cw_ref, cb_ref, gw_ref, gb_ref, lam_ref, h0_ref,
         out_ref, hT_ref, ext_s, a_s, u_s, hs_s, carry_s) = refs
    t = pl.program_id(1)
    ti = nt - 1 - t if reverse else t
    c = v_ref.shape[1]
    s_len = tt // SUBLANES

    @pl.when(t == 0)
    def _():
        carry_s[...] = h0_ref[...]

    ext_s[0:8, :] = jnp.where(ti == 0, 0.0, vp_ref[...])
    ext_s[8:8 + tt, :] = v_ref[...]
    ext_s[8 + tt:16 + tt, :] = jnp.where(ti == nt - 1, 0.0, vn_ref[...])
    vc = cb_ref[...]
    for k in range(CONV_W):
        vc = vc + ext_s[pl.ds(6 + k, tt), :] * cw_ref[k:k + 1, :]

    vb = vc.astype(bf16)
    hw = gw_ref.shape[-1]
    heads = [vb[:, k * hw:(k + 1) * hw] for k in range(c // hw)]
    gate = lambda n: jnp.concatenate([jnp.dot(vh, gw_ref[n, k], preferred_element_type=f32)
                                      for k, vh in enumerate(heads)], axis=1)
    gr = gate(0) + gb_ref[0:1, :]
    gi = gate(1) + gb_ref[1:2, :]
    r = _sigmoid(gr)
    ig = _sigmoid(gi)
    log_a = (-RG_C) * r * _softplus(-lam_ref[...])
    a = jnp.exp(log_a)
    u = jnp.sqrt(-jnp.tanh(log_a) * (a * a + 1.0)) * (ig * vc)

    a_s[...] = a.reshape(SUBLANES, s_len, c)
    u_s[...] = u.reshape(SUBLANES, s_len, c)

    def jj(j):
        return s_len - 1 - j if reverse else j

    def pass1(j, hp):
        h, p = hp
        av = a_s[:, jj(j), :]
        return av * h + u_s[:, jj(j), :], av * p

    hfin, pfin = lax.fori_loop(0, s_len, pass1, (jnp.zeros((SUBLANES, c), f32), jnp.ones((SUBLANES, c), f32)),
                               unroll=8)
    cur = carry_s[...]
    cins = [None] * SUBLANES
    for s in (range(SUBLANES - 1, -1, -1) if reverse else range(SUBLANES)):
        cins[s] = cur
        cur = hfin[s:s + 1, :] + pfin[s:s + 1, :] * cur
    carry_s[...] = cur
    hT_ref[...] = cur

    def pass2(j, h):
        h = a_s[:, jj(j), :] * h + u_s[:, jj(j), :]
        hs_s[:, jj(j), :] = h
        return h

    lax.fori_loop(0, s_len, pass2, jnp.concatenate(cins, axis=0), unroll=8)
    hseq = hs_s[...].reshape(tt, c)
    if final:
        out_ref[...] = ((hf_ref[...] + hseq) * jax.nn.gelu(g_ref[...])).astype(out_ref.dtype)
    else:
        out_ref[...] = hseq


def rg_sweep(gv, conv_w, conv_b, gate_w, gate_b, lam, h0, hf=None, *, reverse, tt=512, heads_per_block=4):
    T, d2 = gv.shape
    d = d2 // 2
    hw = d // RG_HEADS
    c = heads_per_block * hw
    nh = d // c
    tt = _row_tile(T, tt)
    nt = T // tt
    final = hf is not None
    tb = tt // 8
    tix = (lambda t: nt - 1 - t) if reverse else (lambda t: t)
    head_vec = lambda rows: pl.BlockSpec((rows, c), lambda h, t: (0, h))
    in_specs = [
        pl.BlockSpec((tt, c), lambda h, t: (tix(t), nh + h)),
        pl.BlockSpec((8, c), lambda h, t: (jnp.maximum(tix(t) * tb - 1, 0), nh + h)),
        pl.BlockSpec((8, c), lambda h, t: (jnp.minimum((tix(t) + 1) * tb, T // 8 - 1), nh + h)),
        head_vec(CONV_W), head_vec(1),
        pl.BlockSpec((2, heads_per_block, hw, hw), lambda h, t: (0, h, 0, 0)),
        head_vec(2), head_vec(1), head_vec(1),
    ]
    args = [gv, gv, gv, conv_w, conv_b, gate_w, gate_b, lam, h0]
    if final:
        in_specs += [pl.BlockSpec((tt, c), lambda h, t: (tix(t), h)),
                     pl.BlockSpec((tt, c), lambda h, t: (tix(t), h))]
        args += [hf, gv]
    return pl.pallas_call(
        functools.partial(_rg_sweep_kernel, reverse=reverse, final=final, nt=nt, tt=tt),
        grid=(nh, nt),
        in_specs=in_specs,
        out_specs=[pl.BlockSpec((tt, c), lambda h, t: (tix(t), h)), head_vec(1)],
        out_shape=[jax.ShapeDtypeStruct((T, d), bf16 if final else f32), jax.ShapeDtypeStruct((1, d), f32)],
        scratch_shapes=[pltpu.VMEM((tt + 16, c), f32)] + [pltpu.VMEM((SUBLANES, tt // SUBLANES, c), f32)] * 3
        + [pltpu.VMEM((1, c), f32)],
        compiler_params=_cparams("parallel", "arbitrary"),
        name="rg_sweep_bwd" if reverse else "rg_sweep_fwd",
    )(*args)


def rglru_mixer(h_args_l, h_args_c, p):
    w_in = p["w_in"]
    zeros = jnp.zeros((1, w_in.shape[0]), f32)
    gv_c = norm_mod_matmul(*h_args_c, w_in)
    gv_l = norm_mod_matmul(*h_args_l, w_in)
    sw = lambda gv, d, h0, hf, rev: rg_sweep(gv, p["conv_w"], p["conv_b"], p["gate_w"][d], p["gate_b"][d],
                                             p["lam"][d], h0, hf, reverse=rev)
    hf_c, s_c = sw(gv_c, 0, zeros, None, False)
    hf_l, _ = sw(gv_l, 0, s_c, None, False)
    y_c, s_c = sw(gv_c, 1, zeros, hf_c, True)
    y_l, _ = sw(gv_l, 1, s_c, hf_l, True)
    return y_l, y_c


def _router_kernel(x_ref, g_ref, sh_ref, sc_ref, rw_ref, h_ref, aff_ref):
    h_ref[...] = _rms_mod(x_ref[...], g_ref[...], sh_ref[...], sc_ref[...])
    logits = lax.dot_general(rw_ref[...], h_ref[...], (((1,), (1,)), ((), ())), precision=lax.Precision.HIGHEST,
                             preferred_element_type=f32)
    e = jnp.exp(logits - jnp.max(logits, axis=0, keepdims=True))
    aff_ref[...] = e / jnp.sum(e, axis=0, keepdims=True)


def moe_router(x, g, shift, scale, router_wt, *, tm=512):
    m, d = x.shape
    ne = router_wt.shape[0]
    tm = _row_tile(m, tm)
    vec = pl.BlockSpec((1, d), lambda i: (0, 0))
    return pl.pallas_call(
        _router_kernel,
        grid=(m // tm,),
        in_specs=[pl.BlockSpec((tm, d), lambda i: (i, 0)), vec, vec, vec,
                  pl.BlockSpec((ne, d), lambda i: (0, 0))],
        out_specs=[pl.BlockSpec((tm, d), lambda i: (i, 0)),
                   pl.BlockSpec((ne, tm), lambda i: (0, i))],
        out_shape=[jax.ShapeDtypeStruct((m, d), f32),
                   jax.ShapeDtypeStruct((ne, m), f32)],
        compiler_params=_cparams("parallel"),
        name="moe_router",
    )(x, g, shift, scale, router_wt)


def _select_kernel(aff_ref, idx_ref, gate_ref, off_ref, num_ref, cnt_s, *, cap, jb):
    nb = aff_ref.shape[0]
    hi = lax.Precision.HIGHEST
    aff = aff_ref[...]
    keys = pltpu.bitcast(aff, i32)

    def total(m):
        return jnp.sum(jnp.sum(m.astype(i32), axis=0, keepdims=True), axis=1, keepdims=True)

    def bs(i, thr):
        cand = thr | (jnp.int32(1) << (30 - i))
        return jnp.where(total(keys >= cand) >= cap, cand, thr)

    thr = lax.fori_loop(0, 31, bs, jnp.zeros((1, 1), i32))
    gt = keys > thr
    eq = keys == thr
    need = (cap - total(gt)).astype(f32)

    tri = (lax.broadcasted_iota(i32, (LANES, LANES), 0) <= lax.broadcasted_iota(i32, (LANES, LANES), 1)).astype(bf16)
    bi = lax.broadcasted_iota(i32, (nb, nb), 0)
    bj = lax.broadcasted_iota(i32, (nb, nb), 1)

    def block_counts(m):
        within = jnp.dot(m.astype(f32).astype(bf16), tri, preferred_element_type=f32)
        rows = jnp.broadcast_to(within[:, LANES - 1:LANES], (nb, LANES))
        before = jnp.dot((bj < bi).astype(f32), rows, precision=hi, preferred_element_type=f32)
        return within, rows, before

    w_eq, _, b_eq = block_counts(eq)
    sel = gt | (eq & (b_eq + w_eq - eq.astype(f32) < need))
    w_sel, rows, before = block_counts(sel)
    cnt_s[...] = jnp.where(sel, before + w_sel, 0.0)
    off_ref[...] = before[:, 0:1].astype(i32)
    num_ref[...] = rows[:, 0:1].astype(i32)

    sel_b = sel.astype(f32).astype(bf16)
    per_block = lax.dot_general(jnp.ones((SUBLANES, LANES), bf16), sel_b, (((1,), (1,)), ((), ())),
                                preferred_element_type=f32)
    cum_end = jnp.dot(per_block, (bi <= bj).astype(f32), precision=hi, preferred_element_type=f32)[0:1, :]
    lane_b = lax.broadcasted_iota(i32, (jb, nb), 1).astype(f32)
    lane_t = lax.broadcasted_iota(i32, (jb, LANES), 1).astype(f32)
    row = lax.broadcasted_iota(i32, (jb, 1), 0).astype(f32)

    def group(r, carry):
        j0 = r * jb
        slot = row + lax.convert_element_type(j0, f32)
        blk = jnp.sum((cum_end <= slot).astype(f32), axis=1, keepdims=True)
        onehot = (lane_b == blk).astype(f32)
        cnt_rows = jnp.dot(onehot, cnt_s[...], precision=hi, preferred_element_type=f32)
        aff_rows = jnp.dot(onehot, aff_ref[...], precision=hi, preferred_element_type=f32)
        hit = cnt_rows == slot + 1.0
        tok = blk * LANES + jnp.sum(jnp.where(hit, lane_t, 0.0), axis=1, keepdims=True)
        idx_ref[pl.ds(j0, jb), :] = tok.astype(i32)
        gate_ref[pl.ds(j0, jb), :] = jnp.sum(jnp.where(hit, aff_rows, 0.0), axis=1, keepdims=True)
        return carry

    lax.fori_loop(0, cap // jb, group, 0)


def moe_select(aff_t, cap):
    ne, T = aff_t.shape
    jb = min(cap, LANES)
    tile = LANES * LANES
    tp = -(-T // tile) * tile
    aff3 = jnp.pad(aff_t, ((0, 0), (0, tp - T))).reshape(ne, tp // LANES, LANES)
    nb = tp // LANES
    slots = pl.BlockSpec((None, cap, 1), lambda e: (e, 0, 0))
    blocks = pl.BlockSpec((None, nb, 1), lambda e: (e, 0, 0))
    return pl.pallas_call(
        functools.partial(_select_kernel, cap=cap, jb=jb),
        grid=(ne,),
        in_specs=[pl.BlockSpec((None, nb, LANES), lambda e: (e, 0, 0))],
        out_specs=[slots, slots, blocks, blocks],
        out_shape=[jax.ShapeDtypeStruct((ne, cap, 1), i32), jax.ShapeDtypeStruct((ne, cap, 1), f32),
                   jax.ShapeDtypeStruct((ne, nb, 1), i32), jax.ShapeDtypeStruct((ne, nb, 1), i32)],
        scratch_shapes=[pltpu.VMEM((nb, LANES), f32)],
        compiler_params=_cparams("parallel"),
        name="moe_select",
    )(aff3)


def _expert_kernel(idx_ref, gate_ref, h_hbm, wgu_ref, wd_ref, ye_ref, xbuf, sem, *, rows, nsteps):
    step = pl.program_id(0) * pl.num_programs(1) + pl.program_id(1)
    slot = step % 2
    d = wgu_ref.shape[0]
    nq = d // LANES

    def row_copy(stp, slt, r):
        t = idx_ref[stp * rows + r]
        return pltpu.make_async_copy(h_hbm.at[pl.ds(t, 1)], xbuf.at[slt, pl.ds(r, 1)], sem.at[slt])

    def block_wait(slt):
        pltpu.make_async_copy(h_hbm.at[pl.ds(0, rows)], xbuf.at[slt], sem.at[slt]).wait()

    @pl.when(step == 0)
    def _():
        def body(r, c):
            row_copy(step, slot, r).start()
            return c

        lax.fori_loop(0, rows, body, 0, unroll=8)

    nxt = jnp.minimum(step + 1, nsteps - 1)
    for r in range(rows):
        row_copy(nxt, 1 - slot, r).start()

    block_wait(slot)
    x = xbuf[slot].astype(bf16)
    gu = jnp.dot(x, wgu_ref[...], preferred_element_type=f32)
    ff = gu.shape[1] // 2
    hid = (_silu(gu[:, :ff]) * gu[:, ff:]).astype(bf16)
    gate = gate_ref[...]
    nw = SUBLANES * LANES
    for c in range(d // nw):
        y = jnp.dot(hid, wd_ref[:, c * nw:(c + 1) * nw], preferred_element_type=f32) * gate
        y8 = jnp.stack([y[:, i * LANES:(i + 1) * LANES] for i in range(SUBLANES)])
        ye_ref[:, c * SUBLANES:(c + 1) * SUBLANES, :] = jnp.transpose(y8, (1, 0, 2))

    @pl.when(step == nsteps - 1)
    def _():
        block_wait(1 - slot)


def moe_experts(idx_flat, gate, h2, w_gu, w_down, *, rows=256):
    ne, cap, _ = gate.shape
    d, ff = w_down.shape[2], w_down.shape[1]
    nq = d // LANES
    rows = _row_tile(cap, rows)
    nblk = cap // rows
    return pl.pallas_call(
        functools.partial(_expert_kernel, rows=rows, nsteps=ne * nblk),
        grid_spec=pltpu.PrefetchScalarGridSpec(
            num_scalar_prefetch=1,
            grid=(ne, nblk),
            in_specs=[pl.BlockSpec((None, rows, 1), lambda e, j, idx: (e, j, 0)),
                      pl.BlockSpec(memory_space=pl.ANY),
                      pl.BlockSpec((None, d, 2 * ff), lambda e, j, idx: (e, 0, 0), pipeline_mode=pl.Buffered(1)),
                      pl.BlockSpec((None, ff, d), lambda e, j, idx: (e, 0, 0), pipeline_mode=pl.Buffered(1))],
            out_specs=pl.BlockSpec((rows, nq, LANES), lambda e, j, idx: (e * nblk + j, 0, 0)),
            scratch_shapes=[pltpu.VMEM((2, rows, d), f32), pltpu.SemaphoreType.DMA((2,))]),
        out_shape=jax.ShapeDtypeStruct((ne * cap, nq, LANES), f32),
        compiler_params=_cparams("arbitrary", "arbitrary"),
        name="moe_experts",
    )(idx_flat, gate, h2, w_gu, w_down)


COMBINE_WINDOW_LOG2 = 5
COMBINE_WINDOW = 1 << COMBINE_WINDOW_LOG2


def _combine_kernel(idx_ref, off_ref, num_ref, ye_hbm, x_ref, g_ref, gate_ref, o_ref, stage, extra, acc, sem,
                    *, cap, ne, nbk):
    b = pl.program_id(0)
    slot = b % 2
    tb, d = x_ref.shape
    nq = d // LANES
    win = COMBINE_WINDOW

    def window_start(off):
        return jnp.minimum(off, cap - win)

    def start_windows(bb, slt):
        for e in range(ne):
            s0 = window_start(off_ref[e * nbk + bb])
            pltpu.make_async_copy(ye_hbm.at[pl.ds(e * cap + s0, win)], stage.at[slt, pl.ds(e * win, win)],
                                  sem.at[slt]).start()

    @pl.when(b == 0)
    def _():
        start_windows(b, slot)

    @pl.when(b + 1 < nbk)
    def _():
        start_windows(b + 1, 1 - slot)

    pltpu.make_async_copy(ye_hbm.at[pl.ds(0, ne * win)], stage.at[slot], sem.at[slot]).wait()

    acc[...] = jnp.zeros_like(acc)
    for e in range(ne):
        off = off_ref[e * nbk + b]
        n = num_ref[e * nbk + b]
        s0 = window_start(off)
        n_win = jnp.minimum(n, s0 + win - off)

        def add_row(j, carry, e=e, s0=s0):
            t = idx_ref[e * cap + j] - b * tb
            acc[t] = acc[t] + stage[slot, e * win + (j - s0)]
            return carry

        lax.fori_loop(off, off + n_win, add_row, 0)

        def overflow(c, carry, e=e, off=off, n=n, n_win=n_win):
            j0 = off + n_win + c * win
            s1 = window_start(j0)
            cp = pltpu.make_async_copy(ye_hbm.at[pl.ds(e * cap + s1, win)], extra, sem.at[2])
            cp.start()
            cp.wait()

            def add_extra(j, carry2):
                t = idx_ref[e * cap + j] - b * tb
                acc[t] = acc[t] + extra[j - s1]
                return carry2

            lax.fori_loop(j0, jnp.minimum(j0 + win, off + n), add_extra, 0)
            return carry

        lax.fori_loop(0, lax.shift_right_logical(n - n_win + win - 1, COMBINE_WINDOW_LOG2), overflow, 0)

    def pieces(k):
        return jnp.transpose(acc[:, k * SUBLANES:(k + 1) * SUBLANES, :], (1, 0, 2))

    sq = jnp.zeros((tb, LANES), f32)
    for k in range(nq // SUBLANES):
        p8 = pieces(k)
        for i in range(SUBLANES):
            sq = sq + p8[i] * p8[i]
    rinv = lax.rsqrt(jnp.sum(sq, axis=-1, keepdims=True) * (1.0 / d) + EPS)
    for k in range(nq // SUBLANES):
        p8 = pieces(k)
        for i in range(SUBLANES):
            q = k * SUBLANES + i
            sl = slice(q * LANES, (q + 1) * LANES)
            o_ref[:, sl] = x_ref[:, sl] + gate_ref[:, sl] * ((p8[i] * rinv) * g_ref[:, sl])


def moe_combine(idx_flat, off, num, ye, x, g, gate, *, cap):
    T, d = x.shape
    nq = d // LANES
    tb = LANES
    nbk = T // tb
    ne = off.shape[0] // nbk
    win = COMBINE_WINDOW
    assert cap >= win
    vec = pl.BlockSpec((1, d), lambda i, *_: (0, 0))
    return pl.pallas_call(
        functools.partial(_combine_kernel, cap=cap, ne=ne, nbk=nbk),
        grid_spec=pltpu.PrefetchScalarGridSpec(
            num_scalar_prefetch=3,
            grid=(nbk,),
            in_specs=[pl.BlockSpec(memory_space=pl.ANY),
                      pl.BlockSpec((tb, d), lambda i, *_: (i, 0)), vec, vec],
            out_specs=pl.BlockSpec((tb, d), lambda i, *_: (i, 0)),
            scratch_shapes=[pltpu.VMEM((2, ne * win, nq, LANES), f32), pltpu.VMEM((win, nq, LANES), f32),
                            pltpu.VMEM((tb, nq, LANES), f32), pltpu.SemaphoreType.DMA((3,))]),
        out_shape=jax.ShapeDtypeStruct((T, d), f32),
        compiler_params=_cparams("arbitrary"),
        name="moe_combine",
    )(idx_flat, off, num, ye, x, g, gate)


def moe_block(x, g_in, shift, scale, g_out, gate, p):
    T = x.shape[0]
    cap = EC_CAPACITY * T // N_EXPERTS
    nbk = T // LANES
    h3, aff_t = moe_router(x, g_in, shift, scale, p["router_wt"])
    idx, gates, off, num = moe_select(aff_t, cap)
    idx_flat = idx.reshape(-1)
    ye = moe_experts(idx_flat, gates, h3, p["w_gu"], p["w_down"], rows=min(cap, 512))
    return moe_combine(idx_flat, off[:, :nbk, 0].reshape(-1), num[:, :nbk, 0].reshape(-1), ye, x, g_out, gate, cap=cap)


def _conv_silu_kernel(x_ref, xp_ref, xn_ref, cw_ref, cb_ref, o_ref, ext_s, *, nt, tt):
    ti = pl.program_id(0)
    ext_s[0:8, :] = jnp.where(ti == 0, 0.0, xp_ref[...])
    ext_s[8:8 + tt, :] = x_ref[...]
    ext_s[8 + tt:16 + tt, :] = jnp.where(ti == nt - 1, 0.0, xn_ref[...])
    acc = cb_ref[...]
    for k in range(CONV_W):
        acc = acc + ext_s[pl.ds(6 + k, tt), :] * cw_ref[k:k + 1, :]
    o_ref[...] = _silu(acc)


def conv_silu(zx, conv_w, conv_b, col0, *, tt=512, tc=1024):
    T = zx.shape[0]
    n = conv_w.shape[1]
    tt = _row_tile(T, tt)
    nt = T // tt
    tb = tt // 8
    cb0 = col0 // tc
    return pl.pallas_call(
        functools.partial(_conv_silu_kernel, nt=nt, tt=tt),
        grid=(nt, n // tc),
        in_specs=[pl.BlockSpec((tt, tc), lambda t, j: (t, cb0 + j)),
                  pl.BlockSpec((8, tc), lambda t, j: (jnp.maximum(t * tb - 1, 0), cb0 + j)),
                  pl.BlockSpec((8, tc), lambda t, j: (jnp.minimum((t + 1) * tb, T // 8 - 1), cb0 + j)),
                  pl.BlockSpec((CONV_W, tc), lambda t, j: (0, j)),
                  pl.BlockSpec((1, tc), lambda t, j: (0, j))],
        out_specs=pl.BlockSpec((tt, tc), lambda t, j: (t, j)),
        out_shape=jax.ShapeDtypeStruct((T, n), f32),
        scratch_shapes=[pltpu.VMEM((tt + 16, tc), f32)],
        compiler_params=_cparams("parallel", "parallel"),
        name="conv_silu",
    )(zx, zx, zx, conv_w, conv_b)


def _ssd_prep_kernel(raw_ref, bias_ref, alog_ref, dt_ref, ac_ref, act_ref):
    L = SSD_CHUNK
    nh = raw_ref.shape[1] // 2
    hg = nh // SSD_GROUPS
    dt = _softplus(raw_ref[...] + bias_ref[...])
    dta = dt * (-jnp.exp(alog_ref[...]))
    r = lax.broadcasted_iota(i32, (L, L), 0)
    c = lax.broadcasted_iota(i32, (L, L), 1)
    hi = lax.Precision.HIGHEST
    acs = [jnp.dot((c <= r).astype(f32), dta[:, :nh], precision=hi, preferred_element_type=f32),
           jnp.dot((c >= r).astype(f32), dta[:, nh:], precision=hi, preferred_element_type=f32)]
    for d in range(2):
        act = acs[d].T
        for g in range(SSD_GROUPS):
            dt_ref[d, g] = dt[:, d * nh + g * hg:d * nh + (g + 1) * hg]
            ac_ref[d, g] = acs[d][:, g * hg:(g + 1) * hg]
            act_ref[d, g] = act[g * hg:(g + 1) * hg, :]


def ssd_prep(dt_raw, dt_bias, a_log):
    T, nh2 = dt_raw.shape
    nh = nh2 // 2
    hg = nh // SSD_GROUPS
    nc = T // SSD_CHUNK
    vec = pl.BlockSpec((1, nh2), lambda c: (0, 0))
    return pl.pallas_call(
        _ssd_prep_kernel,
        grid=(nc,),
        in_specs=[pl.BlockSpec((SSD_CHUNK, nh2), lambda c: (c, 0)), vec, vec],
        out_specs=[pl.BlockSpec((2, SSD_GROUPS, SSD_CHUNK, hg), lambda c: (0, 0, c, 0)),
                   pl.BlockSpec((2, SSD_GROUPS, SSD_CHUNK, hg), lambda c: (0, 0, c, 0)),
                   pl.BlockSpec((2, SSD_GROUPS, None, hg, SSD_CHUNK), lambda c: (0, 0, c, 0, 0))],
        out_shape=[jax.ShapeDtypeStruct((2, SSD_GROUPS, T, hg), f32),
                   jax.ShapeDtypeStruct((2, SSD_GROUPS, T, hg), f32),
                   jax.ShapeDtypeStruct((2, SSD_GROUPS, nc, hg, SSD_CHUNK), f32)],
        compiler_params=_cparams("parallel"),
        name="ssd_prep",
    )(dt_raw, dt_bias, a_log)


def _ssd_scan_kernel(*refs, reverse, final, hg):
    if final:
        (xs_ref, b_ref, c_ref, dt_ref, ac_ref, act_ref, s0_ref, yf_ref, z_ref, dsk_ref, ng_ref,
         out_ref, st_ref, yz_s) = refs
    else:
        (xs_ref, b_ref, c_ref, dt_ref, ac_ref, act_ref, s0_ref, out_ref, st_ref) = refs
    L = SSD_CHUNK
    P = SSD_HEADDIM
    ci = pl.program_id(0)
    g = pl.program_id(1)

    @pl.when(ci == 0)
    def _():
        st_ref[g] = s0_ref[g]

    xs = xs_ref[...]
    dt16 = dt_ref[...]
    ac16 = ac_ref[...]
    act = act_ref[...]
    expand = (lax.broadcasted_iota(i32, (hg, hg * P), 0)
              == lax.broadcasted_iota(i32, (hg, hg * P), 1) // P).astype(f32)
    dtx = jnp.dot(dt16, expand, precision=lax.Precision.HIGHEST, preferred_element_type=f32)
    xq = (xs * dtx).astype(bf16)
    bm = b_ref[...]
    cm = c_ref[...]
    cb = lax.dot_general(cm.astype(bf16), bm.astype(bf16), (((1,), (1,)), ((), ())), preferred_element_type=f32)
    bt = bm.T
    li = lax.broadcasted_iota(i32, (L, L), 0)
    si = lax.broadcasted_iota(i32, (L, L), 1)
    mask = (li <= si) if reverse else (li >= si)
    last = 0 if reverse else L - 1
    st = st_ref[g]
    stb = st.astype(bf16)
    ys = []
    new_st = []
    for hh in range(hg):
        hs = slice(hh * P, (hh + 1) * P)
        col = ac16[:, hh:hh + 1]
        row = act[hh:hh + 1, :]
        gm = (jnp.exp(jnp.where(mask, col - row, -jnp.inf)) * cb).astype(bf16)
        ce = (cm * jnp.exp(col)).astype(bf16)
        xqh = xq[:, hs]
        ys.append(jnp.dot(jnp.concatenate([gm, ce], axis=1), jnp.concatenate([xqh, stb[:, hs]], axis=0),
                          preferred_element_type=f32))
        tot = row[:, last:last + 1]
        bdt = (bt * jnp.exp(tot - row)).astype(bf16)
        new_st.append(st[:, hs] * jnp.exp(tot) + jnp.dot(bdt, xqh, preferred_element_type=f32))
    y = jnp.concatenate(ys, axis=1)
    st_ref[g] = jnp.concatenate(new_st, axis=1)
    if not final:
        out_ref[...] = y
        return
    yz_s[g] = (yf_ref[...] + y + dsk_ref[...] * xs) * _silu(z_ref[...])

    @pl.when(g == SSD_GROUPS - 1)
    def _():
        w = hg * P
        ss = None
        for k in range(SSD_GROUPS):
            v = yz_s[k]
            s = jnp.sum(v * v, axis=-1, keepdims=True)
            ss = s if ss is None else ss + s
        rinv = lax.rsqrt(ss * (1.0 / (w * SSD_GROUPS)) + EPS)
        for k in range(SSD_GROUPS):
            out_ref[:, k * w:(k + 1) * w] = ((yz_s[k] * rinv) * ng_ref[:, k * w:(k + 1) * w]).astype(out_ref.dtype)


def ssd_scan(xbc, dtg, acg, act, s0, fin=None, *, reverse):
    T = xbc.shape[0]
    L = SSD_CHUNK
    nc = T // L
    hg = dtg.shape[-1]
    w = hg * SSD_HEADDIM
    d_inner = w * SSD_GROUPS
    nb0 = d_inner // SSD_STATE
    final = fin is not None
    cix = (lambda c: nc - 1 - c) if reverse else (lambda c: c)
    full_state = pl.BlockSpec(s0.shape, lambda c, g: (0, 0, 0))
    in_specs = [
        pl.BlockSpec((L, w), lambda c, g: (cix(c), g)),
        pl.BlockSpec((L, SSD_STATE), lambda c, g: (cix(c), nb0 + g)),
        pl.BlockSpec((L, SSD_STATE), lambda c, g: (cix(c), nb0 + SSD_GROUPS + g)),
        pl.BlockSpec((None, L, hg), lambda c, g: (g, cix(c), 0)),
        pl.BlockSpec((None, L, hg), lambda c, g: (g, cix(c), 0)),
        pl.BlockSpec((None, None, hg, L), lambda c, g: (g, cix(c), 0, 0)),
        full_state,
    ]
    args = [xbc, xbc, xbc, dtg, acg, act, s0]
    scratch = []
    if final:
        yf, zx, dsk, ng = fin
        in_specs += [pl.BlockSpec((L, w), lambda c, g: (cix(c), g)),
                     pl.BlockSpec((L, w), lambda c, g: (cix(c), g)),
                     pl.BlockSpec((1, w), lambda c, g: (0, g)),
                     pl.BlockSpec((1, d_inner), lambda c, g: (0, 0))]
        args += [yf, zx, dsk, ng]
        out_spec = pl.BlockSpec((L, d_inner), lambda c, g: (cix(c), 0))
        out_shape = jax.ShapeDtypeStruct((T, d_inner), bf16)
        scratch = [pltpu.VMEM((SSD_GROUPS, L, w), f32)]
    else:
        out_spec = pl.BlockSpec((L, w), lambda c, g: (cix(c), g))
        out_shape = jax.ShapeDtypeStruct((T, d_inner), f32)
    return pl.pallas_call(
        functools.partial(_ssd_scan_kernel, reverse=reverse, final=final, hg=hg),
        grid=(nc, SSD_GROUPS),
        in_specs=in_specs,
        out_specs=[out_spec, full_state],
        out_shape=[out_shape, jax.ShapeDtypeStruct(s0.shape, f32)],
        scratch_shapes=scratch,
        compiler_params=_cparams("arbitrary", "arbitrary"),
        name="ssd_scan_bwd" if reverse else "ssd_scan_fwd",
    )(*args)


def ssd_mixer(h_args_l, h_args_c, p):
    d_inner = p["norm_g"].shape[1]
    hg = d_inner // SSD_HEADDIM // SSD_GROUPS
    s_zero = jnp.zeros((SSD_GROUPS, SSD_STATE, hg * SSD_HEADDIM), f32)

    def prep(h_args):
        zx = norm_mod_matmul(*h_args, p["w_zx"])
        dt_raw = norm_mod_matmul(*h_args, p["w_dt"], tn=p["w_dt"].shape[1])
        xbc = conv_silu(zx, p["conv_w"], p["conv_b"], d_inner)
        return (zx, xbc) + tuple(ssd_prep(dt_raw, p["dt_bias"], p["a_log"]))

    zx_c, xbc_c, dt_c, ac_c, act_c = prep(h_args_c)
    zx_l, xbc_l, dt_l, ac_l, act_l = prep(h_args_l)
    yf_c, s_c = ssd_scan(xbc_c, dt_c[0], ac_c[0], act_c[0], s_zero, reverse=False)
    yf_l, _ = ssd_scan(xbc_l, dt_l[0], ac_l[0], act_l[0], s_c, reverse=False)
    y_c, s_c = ssd_scan(xbc_c, dt_c[1], ac_c[1], act_c[1], s_zero, (yf_c, zx_c, p["d_skip"], p["norm_g"]), reverse=True)
    y_l, _ = ssd_scan(xbc_l, dt_l[1], ac_l[1], act_l[1], s_c, (yf_l, zx_l, p["d_skip"], p["norm_g"]), reverse=True)
    return y_l, y_c


def _rope_table_kernel(cos_ref, sin_ref, *, tt):
    half = HEAD_DIM // 2
    t = lax.broadcasted_iota(i32, (tt, HEAD_DIM), 0) + pl.program_id(0) * tt
    lane = lax.broadcasted_iota(i32, (tt, HEAD_DIM), 1)
    pos = jnp.where(lane < half, t // GRID_W, t % GRID_W).astype(f32)
    k = (lane % (half // 2)).astype(f32)
    inv_freq = jnp.exp(k * (-2.0 / half * math.log(ROPE_BASE)))
    ang = pos * inv_freq
    cos_ref[...] = jnp.cos(ang)
    sin_ref[...] = jnp.where(lane % half < half // 2, -1.0, 1.0) * jnp.sin(ang)


def rope_tables(T, *, tt=512):
    tt = _row_tile(T, tt)
    spec = pl.BlockSpec((tt, HEAD_DIM), lambda i: (i, 0))
    return pl.pallas_call(
        functools.partial(_rope_table_kernel, tt=tt),
        grid=(T // tt,),
        in_specs=[],
        out_specs=[spec, spec],
        out_shape=[jax.ShapeDtypeStruct((T, HEAD_DIM), f32)] * 2,
        compiler_params=_cparams("parallel"),
        name="rope_tables",
    )()


def _rope(x, cos, sin):
    n, w = x.shape
    reps = w // HEAD_DIM
    if reps > 1:
        cos = jnp.concatenate([cos] * reps, axis=1)
        sin = jnp.concatenate([sin] * reps, axis=1)
    q = HEAD_DIM // 4
    lane = lax.broadcasted_iota(i32, (n, w), 1)
    partner = jnp.where(lane % (2 * q) < q, pltpu.roll(x, w - q, axis=1), pltpu.roll(x, q, axis=1))
    return x * cos + partner * sin


def _attn_kernel(*refs, band):
    if band:
        (q_ref, kp_ref, ko_ref, kn_ref, vp_ref, vo_ref, vn_ref, cp_ref, co_ref, cn_ref, sp_ref, so_ref, sn_ref,
         kc_ref, vc_ref, sink_ref, bias_ref, o_ref) = refs
    else:
        q_ref, kc_ref, vc_ref, sink_ref, o_ref = refs
    nrow = q_ref.shape[0]
    q = q_ref[...]
    if band:
        q = _rope(q, co_ref[...], so_ref[...])
    qs = jnp.concatenate([q[:, j * HEAD_DIM:(j + 1) * HEAD_DIM] for j in range(ATTN_GROUP)], axis=0).astype(bf16)
    kc = kc_ref[...].astype(bf16)
    vc = vc_ref[...].astype(bf16)
    nt = (((1,), (1,)), ((), ()))
    scale = HEAD_DIM ** -0.5
    sink = sink_ref[...]
    s_c = lax.dot_general(qs, kc, nt, preferred_element_type=f32) * scale
    m = jnp.maximum(jnp.max(s_c, axis=1, keepdims=True), sink)
    if band:
        kb = jnp.concatenate([_rope(kp_ref[...], cp_ref[...], sp_ref[...]),
                              _rope(ko_ref[...], co_ref[...], so_ref[...]),
                              _rope(kn_ref[...], cn_ref[...], sn_ref[...])], axis=0).astype(bf16)
        vb = jnp.concatenate([vp_ref[...], vo_ref[...], vn_ref[...]], axis=0).astype(bf16)
        s_b = lax.dot_general(qs, kb, nt, preferred_element_type=f32) * scale \
            + jnp.concatenate([bias_ref[...]] * ATTN_GROUP, axis=0)
        m =jnp.maximum(m, jnp.max(s_b, axis=1, keepdims=True))
        p_b = jnp.exp(s_b - m)
    p_c = jnp.exp(s_c - m)
    den = jnp.sum(p_c, axis=1, keepdims=True) + jnp.exp(sink - m)
    if band:
        den = den + jnp.sum(p_b, axis=1, keepdims=True)
    rden = 1.0 / den
    o = jnp.dot((p_c * rden).astype(bf16), vc, preferred_element_type=f32)
    if band:
        o = o + jnp.dot((p_b * rden).astype(bf16), vb, preferred_element_type=f32)
    o_ref[...] = jnp.concatenate([o[j * nrow:(j + 1) * nrow, :] for j in range(ATTN_GROUP)], axis=1).astype(o_ref.dtype)


def window_attention(qkv_l, qkv_c, sink_rows, cos, sin):
    T = qkv_l.shape[0]
    Lc = qkv_c.shape[0]
    nkv = qkv_l.shape[1] // HEAD_DIM // (ATTN_GROUP + 2)
    nh = nkv * ATTN_GROUP
    k0, v0 = nh, nh + nkv
    gw = ATTN_GROUP * HEAD_DIM
    qb = ATTN_QBLOCKS
    blk = qb * WINDOW
    nq = T // blk
    nw = T // WINDOW
    rows = [(WINDOW, lambda i: jnp.maximum(qb * i - 1, 0)), (blk, lambda i: i),
            (WINDOW, lambda i: jnp.minimum(qb * (i + 1), nw - 1))]
    kv = lambda c0: [pl.BlockSpec((r, HEAD_DIM), (lambda g, i, f=f, c0=c0: (f(i), c0 + g))) for r, f in rows]
    tab = [pl.BlockSpec((r, HEAD_DIM), (lambda g, i, f=f: (f(i), 0))) for r, f in rows]
    ctx_specs = lambda: [pl.BlockSpec((Lc, HEAD_DIM), lambda g, i: (0, k0 + g)),
                         pl.BlockSpec((Lc, HEAD_DIM), lambda g, i: (0, v0 + g))]
    r = jnp.arange(blk)[:, None] + WINDOW
    c = jnp.arange(blk + 2 * WINDOW)[None, :]

    def bias_of(i):
        k_abs = c + (i * blk - WINDOW)
        return jnp.where((jnp.abs(r - c) <= WINDOW) & (k_abs >= 0) & (k_abs < T), 0.0, -jnp.inf).astype(f32)

    bias = jnp.stack([bias_of(0), bias_of(min(1, nq - 1)), bias_of(nq - 1)])
    which = lambda i: jnp.where(i == 0, 0, jnp.where(i == nq - 1, 2, 1))
    o_l = pl.pallas_call(
        functools.partial(_attn_kernel, band=True),
        grid=(nkv, nq),
        in_specs=[pl.BlockSpec((blk, gw), lambda g, i: (i, g))] + kv(k0) + kv(v0) + tab + tab + ctx_specs()
        + [pl.BlockSpec((None, ATTN_GROUP * blk, 1), lambda g, i: (g, 0, 0)),
           pl.BlockSpec((None, blk, blk + 2 * WINDOW), lambda g, i: (which(i), 0, 0))],
        out_specs=pl.BlockSpec((blk, gw), lambda g, i: (i, g)),
        out_shape=jax.ShapeDtypeStruct((T, nh * HEAD_DIM), bf16),
        compiler_params=_cparams("parallel", "parallel"),
        name="window_attention",
    )(qkv_l, *([qkv_l] * 6), cos, cos, cos, sin, sin, sin, qkv_c, qkv_c, sink_rows[0], bias)
    o_c = pl.pallas_call(
        functools.partial(_attn_kernel, band=False),
        grid=(nkv, 1),
        in_specs=[pl.BlockSpec((Lc, gw), lambda g, i: (0, g))] + ctx_specs()
        + [pl.BlockSpec((None, ATTN_GROUP * Lc, 1), lambda g, i: (g, 0, 0))],
        out_specs=pl.BlockSpec((Lc, gw), lambda g, i: (0, g)),
        out_shape=jax.ShapeDtypeStruct((Lc, nh * HEAD_DIM), bf16),
        compiler_params=_cparams("parallel", "parallel"),
        name="context_attention",
    )(qkv_c, qkv_c, qkv_c, sink_rows[1])
    return o_l, o_c


def _adaln_kernel(c_ref, dn_ref, up_ref, b_ref, o_ref):
    hi = lax.Precision.HIGHEST
    t = jnp.dot(jax.nn.silu(c_ref[...]), dn_ref[...], precision=hi, preferred_element_type=f32)
    o_ref[...] = jnp.dot(t, up_ref[...], precision=hi, preferred_element_type=f32) + b_ref[...]


def adaln(cond, w_down, w_up, b, *, tn=2048):
    depth, d, r = w_down.shape
    n = w_up.shape[2]
    rows = cond.shape[0]
    return pl.pallas_call(
        _adaln_kernel,
        grid=(depth, n // tn),
        in_specs=[pl.BlockSpec((rows, d), lambda l, j: (0, 0)),
                  pl.BlockSpec((None, d, r), lambda l, j: (l, 0, 0)),
                  pl.BlockSpec((None, r, tn), lambda l, j: (l, 0, j)),
                  pl.BlockSpec((None, 1, tn), lambda l, j: (l, 0, j))],
        out_specs=pl.BlockSpec((None, rows, tn), lambda l, j: (l, 0, j)),
        out_shape=jax.ShapeDtypeStruct((depth, rows, n), f32),
        compiler_params=_cparams("parallel", "parallel"),
        name="adaln",
    )(cond, w_down, w_up, b.reshape(depth, 1, n))


def kernel(x, c, ctx, c_ctx, ada_down, ada_up, ada_b, norm_g, router_w, moe_w_gate, moe_w_up, moe_w_down,
           rg_w_in, rg_conv_w, rg_conv_b, rg_gate_w, rg_gate_b, rg_lambda, rg_w_out,
           ssd_w_in, ssd_conv_w, ssd_conv_b, ssd_dt_bias, ssd_a_log, ssd_d, ssd_norm_g, ssd_w_out,
           attn_w_qkv, attn_sink, attn_w_out):
    assert x.shape[0] == 1 and ctx.shape[0] == 1 and c.shape[0] == 1
    depth = ada_down.shape[0]
    T, d = x.shape[1], x.shape[2]
    Lc = ctx.shape[1]
    xl, xc = x[0], ctx[0]
    cond = jnp.concatenate([c, c_ctx[None, :], jnp.zeros((SUBLANES - 2, d), f32)], axis=0)
    mods = adaln(cond, ada_down, ada_up, ada_b)
    row = lambda v: v.reshape(1, -1)

    for i in range(depth):
        kind, slot = i % N_MIXERS, i // N_MIXERS
        ml = [mods[i, 0:1, k * d:(k + 1) * d] for k in range(N_MOD)]
        mc = [mods[i, 1:2, k * d:(k + 1) * d] for k in range(N_MOD)]
        g = [row(norm_g[i, k]) for k in range(4)]
        last = i == depth - 1
        in_l = (xl, g[0], ml[0], ml[1])
        in_c = (xc, g[0], mc[0], mc[1])
        if kind == 0:
            p = dict(w_in=to_bf16(rg_w_in, layer=slot), conv_w=rg_conv_w[slot], conv_b=row(rg_conv_b[slot]),
                     gate_w=to_bf16(rg_gate_w, layer=slot), gate_b=rg_gate_b[slot],
                     lam=rg_lambda[slot].reshape(2, 1, -1))
            y_l, y_c = rglru_mixer(in_l, in_c, p)
            w_out = to_bf16(rg_w_out, layer=slot)
        elif kind == 1:
            d_inner = ssd_norm_g.shape[1]
            n_zx = d_inner + ssd_conv_w.shape[2]
            p = dict(w_zx=to_bf16(ssd_w_in, layer=slot, col0=0, ncols=n_zx),
                     w_dt=to_bf16(ssd_w_in, layer=slot, col0=n_zx, ncols=ssd_w_in.shape[2] - n_zx),
                     conv_w=ssd_conv_w[slot], conv_b=row(ssd_conv_b[slot]),
                     dt_bias=row(ssd_dt_bias[slot]), a_log=row(ssd_a_log[slot]),
                     d_skip=row(jnp.repeat(ssd_d[slot], SSD_HEADDIM)), norm_g=row(ssd_norm_g[slot]))
            y_l, y_c = ssd_mixer(in_l, in_c, p)
            w_out = to_bf16(ssd_w_out, layer=slot)
        else:
            w_qkv = to_bf16(attn_w_qkv, layer=slot)
            qkv_l = norm_mod_matmul(*in_l, w_qkv)
            qkv_c = norm_mod_matmul(*in_c, w_qkv)
            sk = attn_sink[slot].reshape(-1, ATTN_GROUP, 1, 1)
            sink_rows = [jnp.broadcast_to(sk, sk.shape[:2] + (n, 1)).reshape(sk.shape[0], ATTN_GROUP * n, 1)
                         for n in (ATTN_QBLOCKS * WINDOW, Lc)]
            cos, sin = rope_tables(T)
            y_l, y_c = window_attention(qkv_l, qkv_c, sink_rows, cos, sin)
            w_out = to_bf16(attn_w_out, layer=slot)
        mp = dict(router_wt=router_w[i].T, w_gu=to_bf16(moe_w_gate, moe_w_up, layer=i),
                  w_down=to_bf16(moe_w_down, layer=i))
        xl = matmul_norm_res(y_l, w_out, g[1], ml[2], xl)
        xl = moe_block(xl, g[2], ml[3], ml[4], g[3], ml[5], mp)
        if not last:
            xc = matmul_norm_res(y_c, w_out, g[1], mc[2], xc)
            xc = moe_block(xc, g[2], mc[3], mc[4], g[3], mc[5], mp)
    return xl[None]
```

```python
import functools
import math

import jax
import jax.numpy as jnp
from jax import lax
from jax.experimental import pallas as pl
from jax.experimental.pallas import tpu as pltpu

f32 = jnp.float32
bf16 = jnp.bfloat16
i32 = jnp.int32

EPS = 1e-6
N_MIXERS = 3
N_MOD = 6
CONV_W = 4
RG_HEADS = 16
RG_C = 8.0
SSD_HEADDIM = 64
SSD_STATE = 128
SSD_GROUPS = 8
SSD_CHUNK = 128
HEAD_DIM = 128
ATTN_GROUP = 4
WINDOW = 128
ATTN_QBLOCKS = 2
ROPE_BASE = 10000.0
GRID_W = 64
N_EXPERTS = 16
EC_CAPACITY = 2

LANES = 128
SUBLANES = 8
VMEM_LIMIT = 56 * 1024 * 1024


def _cparams(*sem):
    return pltpu.CompilerParams(dimension_semantics=sem, vmem_limit_bytes=VMEM_LIMIT)


def _row_tile(m, want):
    t = min(m, want)
    assert m % t == 0
    return t


def _sigmoid(x):
    return 0.5 * jnp.tanh(0.5 * x) + 0.5


def _silu(x):
    return x * _sigmoid(x)


def _rms_mod(x, g, shift, scale):
    y = x * lax.rsqrt(jnp.mean(x * x, axis=-1, keepdims=True) + EPS)
    return (y * g) * (1.0 + scale) + shift


CAST_BLOCK_BYTES = 4 * 1024 * 1024


def _cast_kernel(*refs):
    *ins, o_ref = refs
    col = 0
    for r in ins:
        w = r.shape[-1]
        o_ref[:, col:col + w] = r[...].astype(o_ref.dtype)
        col += w


def to_bf16(*ws, layer, col0=0, ncols=None):
    lead = ws[0].shape[1:-1]
    xs = [w.reshape(-1, w.shape[-1]) for w in ws]
    m = xs[0].shape[0] // ws[0].shape[0]
    if len(xs) > 1 or ncols is None:
        assert col0 == 0 and ncols is None
        tcs = [x.shape[1] for x in xs]
        n = sum(tcs)
        nj = 1
    else:
        n = ncols
        tc = 2048 if (n % 2048 == 0 and col0 % 2048 == 0) else n
        assert col0 % tc == 0
        tcs = [tc]
        nj = n // tc
    if nj == 1 and len(xs) == 1 and n % 2048 == 0 and n > 2048:
        tcs, nj = [2048], n // 2048
    want = max(SUBLANES, CAST_BLOCK_BYTES // (4 * sum(tcs)))
    tr = _row_tile(m, 1 << (want.bit_length() - 1))
    c0 = col0 // tcs[0]
    r0 = layer * (m // tr)
    out = pl.pallas_call(
        _cast_kernel,
        grid=(m // tr, nj),
        in_specs=[pl.BlockSpec((tr, tc), lambda i, j: (i + r0, j + c0)) for tc in tcs],
        out_specs=pl.BlockSpec((tr, sum(tcs)), lambda i, j: (i, j)),
        out_shape=jax.ShapeDtypeStruct((m, n), bf16),
        compiler_params=_cparams("parallel", "parallel"),
        name="to_bf16",
    )(*xs)
    return out.reshape(lead + (n,))


def _nmm_kernel(x_ref, g_ref, sh_ref, sc_ref, w_ref, o_ref, a_ref):
    @pl.when(pl.program_id(1) == 0)
    def _():
        a_ref[...] = _rms_mod(x_ref[...], g_ref[...], sh_ref[...], sc_ref[...]).astype(a_ref.dtype)

    o_ref[...] = jnp.dot(a_ref[...], w_ref[...], preferred_element_type=f32).astype(o_ref.dtype)


def norm_mod_matmul(x, g, shift, scale, w, *, tm=512, tn=1024, out_dtype=f32):
    m, d = x.shape
    n = w.shape[1]
    tm = _row_tile(m, tm)
    tn = _row_tile(n, tn)
    vec = pl.BlockSpec((1, d), lambda i, j: (0, 0))
    return pl.pallas_call(
        _nmm_kernel,
        grid=(m // tm, n // tn),
        in_specs=[pl.BlockSpec((tm, d), lambda i, j: (i, 0)), vec, vec, vec,
                  pl.BlockSpec((d, tn), lambda i, j: (0, j))],
        out_specs=pl.BlockSpec((tm, tn), lambda i, j: (i, j)),
        out_shape=jax.ShapeDtypeStruct((m, n), out_dtype),
        scratch_shapes=[pltpu.VMEM((tm, d), bf16)],
        compiler_params=_cparams("parallel", "arbitrary"),
        name="norm_mod_matmul",
    )(x, g, shift, scale, w)


def _mnr_kernel(a_ref, w_ref, g_ref, gate_ref, r_ref, o_ref, y_ref, ss_ref, *, nj, n_total):
    j = pl.program_id(1)

    @pl.when(j == 0)
    def _():
        ss_ref[...] = jnp.zeros_like(ss_ref)

    @pl.when(j < nj)
    def _():
        y = jnp.dot(a_ref[...], w_ref[...], preferred_element_type=f32)
        y_ref[j] = y
        ss_ref[...] += jnp.sum(y * y, axis=-1, keepdims=True)

    @pl.when(j >= nj)
    def _():
        rinv = lax.rsqrt(ss_ref[...] * (1.0 / n_total) + EPS)
        o_ref[...] = r_ref[...] + gate_ref[...] * ((y_ref[j - nj] * rinv) * g_ref[...])


def matmul_norm_res(a, w, g, gate, resid, *, tn=512):
    m, k = a.shape
    n = w.shape[1]
    tm = _row_tile(m, 4 * 1024 * 1024 // k)
    tn = _row_tile(n, tn)
    nj = n // tn
    ph2 = lambda j: jnp.maximum(j - nj, 0)
    return pl.pallas_call(
        functools.partial(_mnr_kernel, nj=nj, n_total=n),
        grid=(m // tm, 2 * nj),
        in_specs=[pl.BlockSpec((tm, k), lambda i, j: (i, 0)),
                  pl.BlockSpec((k, tn), lambda i, j: (0, jnp.minimum(j, nj - 1))),
                  pl.BlockSpec((1, tn), lambda i, j: (0, ph2(j))),
                  pl.BlockSpec((1, tn), lambda i, j: (0, ph2(j))),
                  pl.BlockSpec((tm, tn), lambda i, j: (i, ph2(j)))],
        out_specs=pl.BlockSpec((tm, tn), lambda i, j: (i, ph2(j))),
        out_shape=jax.ShapeDtypeStruct((m, n), f32),
        scratch_shapes=[pltpu.VMEM((nj, tm, tn), f32), pltpu.VMEM((tm, 1), f32)],
        compiler_params=_cparams("parallel", "arbitrary"),
        name="matmul_norm_res",
    )(a, w, g, gate, resid)


def _softplus(x):
    return jnp.maximum(x, 0.0) + jnp.log1p(jnp.exp(-jnp.abs(x)))


def _rg_sweep_kernel(*refs, reverse, final, nt, tt):
    if final:
        (v_ref, vp_ref, vn_ref, cw_ref, cb_ref, gw_ref, gb_ref, lam_ref, h0_ref, hf_ref, g_ref,
         out_ref, hT_ref, ext_s, a_s, u_s, hs_s, carry_s) = refs
    else:
        (v_ref, vp_ref, vn_ref, cw_ref, cb_ref, gw_ref, gb_ref, lam_ref, h0_ref,
         out_ref, hT_ref, ext_s, a_s, u_s, hs_s, carry_s) = refs
    t = pl.program_id(1)
    ti = nt - 1 - t if reverse else t
    c = v_ref.shape[1]
    s_len = tt // SUBLANES

    @pl.when(t == 0)
    def _():
        carry_s[...] = h0_ref[...]

    ext_s[0:8, :] = jnp.where(ti == 0, 0.0, vp_ref[...])
    ext_s[8:8 + tt, :] = v_ref[...]
    ext_s[8 + tt:16 + tt, :] = jnp.where(ti == nt - 1, 0.0, vn_ref[...])
    vc = cb_ref[...]
    for k in range(CONV_W):
        vc = vc + ext_s[pl.ds(6 + k, tt), :] * cw_ref[k:k + 1, :]

    vb = vc.astype(bf16)
    hw = gw_ref.shape[-1]
    heads = [vb[:, k * hw:(k + 1) * hw] for k in range(c // hw)]
    gate = lambda n: jnp.concatenate([jnp.dot(vh, gw_ref[n, k], preferred_element_type=f32)
                                      for k, vh in enumerate(heads)], axis=1)
    gr = gate(0) + gb_ref[0:1, :]
    gi = gate(1) + gb_ref[1:2, :]
    r = _sigmoid(gr)
    ig = _sigmoid(gi)
    log_a = (-RG_C) * r * _softplus(-lam_ref[...])
    a = jnp.exp(log_a)
    u = jnp.sqrt(-jnp.tanh(log_a) * (a * a + 1.0)) * (ig * vc)

    a_s[...] = jnp.transpose(a.reshape(SUBLANES, s_len, c), (1, 0, 2))
    u_s[...] = jnp.transpose(u.reshape(SUBLANES, s_len, c), (1, 0, 2))

    def jj(j):
        return s_len - 1 - j if reverse else j

    def pass1(j, hp):
        h, p = hp
        av = a_s[jj(j)]
        return av * h + u_s[jj(j)], av * p

    hfin, pfin = lax.fori_loop(0, s_len, pass1, (jnp.zeros((SUBLANES, c), f32), jnp.ones((SUBLANES, c), f32)),
                               unroll=8)
    cur = carry_s[...]
    cins = [None] * SUBLANES
    for s in (range(SUBLANES - 1, -1, -1) if reverse else range(SUBLANES)):
        cins[s] = cur
        cur = hfin[s:s + 1, :] + pfin[s:s + 1, :] * cur
    carry_s[...] = cur
    hT_ref[...] = cur

    def pass2(j, h):
        h = a_s[jj(j)] * h + u_s[jj(j)]
        hs_s[jj(j)] = h
        return h

    lax.fori_loop(0, s_len, pass2, jnp.concatenate(cins, axis=0), unroll=8)
    hseq = jnp.transpose(hs_s[...], (1, 0, 2)).reshape(tt, c)
    if final:
        out_ref[...] = ((hf_ref[...] + hseq) * jax.nn.gelu(g_ref[...])).astype(out_ref.dtype)
    else:
        out_ref[...] = hseq


def rg_sweep(gv, conv_w, conv_b, gate_w, gate_b, lam, h0, hf=None, *, reverse, tt=512, heads_per_block=4):
    T, d2 = gv.shape
    d = d2 // 2
    hw = d // RG_HEADS
    c = heads_per_block * hw
    nh = d // c
    tt = _row_tile(T, tt)
    nt = T // tt
    final = hf is not None
    tb = tt // 8
    tix = (lambda t: nt - 1 - t) if reverse else (lambda t: t)
    head_vec = lambda rows: pl.BlockSpec((rows, c), lambda h, t: (0, h))
    in_specs = [
        pl.BlockSpec((tt, c), lambda h, t: (tix(t), nh + h)),
        pl.BlockSpec((8, c), lambda h, t: (jnp.maximum(tix(t) * tb - 1, 0), nh + h)),
        pl.BlockSpec((8, c), lambda h, t: (jnp.minimum((tix(t) + 1) * tb, T // 8 - 1), nh + h)),
        head_vec(CONV_W), head_vec(1),
        pl.BlockSpec((2, heads_per_block, hw, hw), lambda h, t: (0, h, 0, 0)),
        head_vec(2), head_vec(1), head_vec(1),
    ]
    args = [gv, gv, gv, conv_w, conv_b, gate_w, gate_b, lam, h0]
    if final:
        in_specs += [pl.BlockSpec((tt, c), lambda h, t: (tix(t), h)),
                     pl.BlockSpec((tt, c), lambda h, t: (tix(t), h))]
        args += [hf, gv]
    return pl.pallas_call(
        functools.partial(_rg_sweep_kernel, reverse=reverse, final=final, nt=nt, tt=tt),
        grid=(nh, nt),
        in_specs=in_specs,
        out_specs=[pl.BlockSpec((tt, c), lambda h, t: (tix(t), h)), head_vec(1)],
        out_shape=[jax.ShapeDtypeStruct((T, d), bf16 if final else f32), jax.ShapeDtypeStruct((1, d), f32)],
        scratch_shapes=[pltpu.VMEM((tt + 16, c), f32)] + [pltpu.VMEM((tt // SUBLANES, SUBLANES, c), f32)] * 3
        + [pltpu.VMEM((1, c), f32)],
        compiler_params=_cparams("parallel", "arbitrary"),
        name="rg_sweep_bwd" if reverse else "rg_sweep_fwd",
    )(*args)


def rglru_mixer(h_args_l, h_args_c, p):
    w_in = p["w_in"]
    zeros = jnp.zeros((1, w_in.shape[0]), f32)
    gv_c = norm_mod_matmul(*h_args_c, w_in)
    gv_l = norm_mod_matmul(*h_args_l, w_in)
    sw = lambda gv, d, h0, hf, rev: rg_sweep(gv, p["conv_w"], p["conv_b"], p["gate_w"][d], p["gate_b"][d],
                                             p["lam"][d], h0, hf, reverse=rev)
    hf_c, s_c = sw(gv_c, 0, zeros, None, False)
    hf_l, _ = sw(gv_l, 0, s_c, None, False)
    y_c, s_c = sw(gv_c, 1, zeros, hf_c, True)
    y_l, _ = sw(gv_l, 1, s_c, hf_l, True)
    return y_l, y_c


def _router_kernel(x_ref, g_ref, sh_ref, sc_ref, rw_ref, h_ref, aff_ref):
    h_ref[...] = _rms_mod(x_ref[...], g_ref[...], sh_ref[...], sc_ref[...])
    logits = lax.dot_general(rw_ref[...], h_ref[...], (((1,), (1,)), ((), ())), precision=lax.Precision.HIGHEST,
                             preferred_element_type=f32)
    e = jnp.exp(logits - jnp.max(logits, axis=0, keepdims=True))
    aff_ref[...] = e / jnp.sum(e, axis=0, keepdims=True)


def moe_router(x, g, shift, scale, router_wt, *, tm=512):
    m, d = x.shape
    ne = router_wt.shape[0]
    tm = _row_tile(m, tm)
    vec = pl.BlockSpec((1, d), lambda i: (0, 0))
    return pl.pallas_call(
        _router_kernel,
        grid=(m // tm,),
        in_specs=[pl.BlockSpec((tm, d), lambda i: (i, 0)), vec, vec, vec,
                  pl.BlockSpec((ne, d), lambda i: (0, 0))],
        out_specs=[pl.BlockSpec((tm, d), lambda i: (i, 0)),
                   pl.BlockSpec((ne, tm), lambda i: (0, i))],
        out_shape=[jax.ShapeDtypeStruct((m, d), f32),
                   jax.ShapeDtypeStruct((ne, m), f32)],
        compiler_params=_cparams("parallel"),
        name="moe_router",
    )(x, g, shift, scale, router_wt)


def _select_kernel(aff_ref, idx_ref, gate_ref, off_ref, num_ref, cnt_s, *, cap, jb):
    nb = aff_ref.shape[0]
    hi = lax.Precision.HIGHEST
    aff = aff_ref[...]
    keys = pltpu.bitcast(aff, i32)

    def total(m):
        return jnp.sum(jnp.sum(m.astype(i32), axis=0, keepdims=True), axis=1, keepdims=True)

    def bs(i, thr):
        cand = thr | (jnp.int32(1) << (30 - i))
        return jnp.where(total(keys >= cand) >= cap, cand, thr)

    thr = lax.fori_loop(0, 31, bs, jnp.zeros((1, 1), i32))
    gt = keys > thr
    eq = keys == thr
    need = (cap - total(gt)).astype(f32)

    tri = (lax.broadcasted_iota(i32, (LANES, LANES), 0) <= lax.broadcasted_iota(i32, (LANES, LANES), 1)).astype(bf16)
    bi = lax.broadcasted_iota(i32, (nb, nb), 0)
    bj = lax.broadcasted_iota(i32, (nb, nb), 1)

    def block_counts(m):
        within = jnp.dot(m.astype(f32).astype(bf16), tri, preferred_element_type=f32)
        rows = jnp.broadcast_to(within[:, LANES - 1:LANES], (nb, LANES))
        before = jnp.dot((bj < bi).astype(f32), rows, precision=hi, preferred_element_type=f32)
        return within, rows, before

    w_eq, _, b_eq = block_counts(eq)
    sel = gt | (eq & (b_eq + w_eq - eq.astype(f32) < need))
    w_sel, rows, before = block_counts(sel)
    cnt_s[...] = jnp.where(sel, before + w_sel, 0.0)
    off_ref[...] = before[:, 0:1].astype(i32)
    num_ref[...] = rows[:, 0:1].astype(i32)

    sel_b = sel.astype(f32).astype(bf16)
    per_block = lax.dot_general(jnp.ones((SUBLANES, LANES), bf16), sel_b, (((1,), (1,)), ((), ())),
                                preferred_element_type=f32)
    cum_end = jnp.dot(per_block, (bi <= bj).astype(f32), precision=hi, preferred_element_type=f32)[0:1, :]
    lane_b = lax.broadcasted_iota(i32, (jb, nb), 1).astype(f32)
    lane_t = lax.broadcasted_iota(i32, (jb, LANES), 1).astype(f32)
    row = lax.broadcasted_iota(i32, (jb, 1), 0).astype(f32)

    def group(r, carry):
        j0 = r * jb
        slot = row + lax.convert_element_type(j0, f32)
        blk = jnp.sum((cum_end <= slot).astype(f32), axis=1, keepdims=True)
        onehot = (lane_b == blk).astype(f32)
        cnt_rows = jnp.dot(onehot, cnt_s[...], precision=hi, preferred_element_type=f32)
        aff_rows = jnp.dot(onehot, aff_ref[...], precision=hi, preferred_element_type=f32)
        hit = cnt_rows == slot + 1.0
        tok = blk * LANES + jnp.sum(jnp.where(hit, lane_t, 0.0), axis=1, keepdims=True)
        idx_ref[pl.ds(j0, jb), :] = tok.astype(i32)
        gate_ref[pl.ds(j0, jb), :] = jnp.sum(jnp.where(hit, aff_rows, 0.0), axis=1, keepdims=True)
        return carry

    lax.fori_loop(0, cap // jb, group, 0)


def moe_select(aff_t, cap):
    ne, T = aff_t.shape
    jb = min(cap, LANES)
    tile = SUBLANES * LANES
    tp = -(-T // tile) * tile
    aff3 = jnp.pad(aff_t, ((0, 0), (0, tp - T))).reshape(ne, tp // LANES, LANES)
    nb = tp // LANES
    slots = pl.BlockSpec((None, cap, 1), lambda e: (e, 0, 0))
    blocks = pl.BlockSpec((None, nb, 1), lambda e: (e, 0, 0))
    return pl.pallas_call(
        functools.partial(_select_kernel, cap=cap, jb=jb),
        grid=(ne,),
        in_specs=[pl.BlockSpec((None, nb, LANES), lambda e: (e, 0, 0))],
        out_specs=[slots, slots, blocks, blocks],
        out_shape=[jax.ShapeDtypeStruct((ne, cap, 1), i32), jax.ShapeDtypeStruct((ne, cap, 1), f32),
                   jax.ShapeDtypeStruct((ne, nb, 1), i32), jax.ShapeDtypeStruct((ne, nb, 1), i32)],
        scratch_shapes=[pltpu.VMEM((nb, LANES), f32)],
        compiler_params=_cparams("parallel"),
        name="moe_select",
    )(aff3)


def _expert_kernel(idx_ref, gate_ref, h_hbm, wgu_ref, wd_ref, ye_ref, xbuf, sem, *, rows, nsteps):
    step = pl.program_id(0) * pl.num_programs(1) + pl.program_id(1)
    slot = step % 2
    d = wgu_ref.shape[0]
    nq = d // LANES

    def row_copy(stp, slt, r):
        t = idx_ref[stp * rows + r]
        return pltpu.make_async_copy(h_hbm.at[pl.ds(t, 1)], xbuf.at[slt, pl.ds(r, 1)], sem.at[slt])

    def block_wait(slt):
        pltpu.make_async_copy(h_hbm.at[pl.ds(0, rows)], xbuf.at[slt], sem.at[slt]).wait()

    @pl.when(step == 0)
    def _():
        def body(r, c):
            row_copy(step, slot, r).start()
            return c

        lax.fori_loop(0, rows, body, 0, unroll=8)

    nxt = jnp.minimum(step + 1, nsteps - 1)
    for r in range(rows):
        row_copy(nxt, 1 - slot, r).start()

    block_wait(slot)
    x = xbuf[slot].astype(bf16)
    gu = jnp.dot(x, wgu_ref[...], preferred_element_type=f32)
    ff = gu.shape[1] // 2
    hid = (_silu(gu[:, :ff]) * gu[:, ff:]).astype(bf16)
    gate = gate_ref[...]
    nw = SUBLANES * LANES
    for c in range(d // nw):
        y = jnp.dot(hid, wd_ref[:, c * nw:(c + 1) * nw], preferred_element_type=f32) * gate
        y8 = jnp.stack([y[:, i * LANES:(i + 1) * LANES] for i in range(SUBLANES)])
        ye_ref[:, c * SUBLANES:(c + 1) * SUBLANES, :] = jnp.transpose(y8, (1, 0, 2))

    @pl.when(step == nsteps - 1)
    def _():
        block_wait(1 - slot)


def moe_experts(idx_flat, gate, h2, w_gu, w_down, *, rows=256):
    ne, cap, _ = gate.shape
    d, ff = w_down.shape[2], w_down.shape[1]
    nq = d // LANES
    rows = _row_tile(cap, rows)
    nblk = cap // rows
    return pl.pallas_call(
        functools.partial(_expert_kernel, rows=rows, nsteps=ne * nblk),
        grid_spec=pltpu.PrefetchScalarGridSpec(
            num_scalar_prefetch=1,
            grid=(ne, nblk),
            in_specs=[pl.BlockSpec((None, rows, 1), lambda e, j, idx: (e, j, 0)),
                      pl.BlockSpec(memory_space=pl.ANY),
                      pl.BlockSpec((None, d, 2 * ff), lambda e, j, idx: (e, 0, 0), pipeline_mode=pl.Buffered(1)),
                      pl.BlockSpec((None, ff, d), lambda e, j, idx: (e, 0, 0), pipeline_mode=pl.Buffered(1))],
            out_specs=pl.BlockSpec((rows, nq, LANES), lambda e, j, idx: (e * nblk + j, 0, 0)),
            scratch_shapes=[pltpu.VMEM((2, rows, d), f32), pltpu.SemaphoreType.DMA((2,))]),
        out_shape=jax.ShapeDtypeStruct((ne * cap, nq, LANES), f32),
        compiler_params=_cparams("arbitrary", "arbitrary"),
        name="moe_experts",
    )(idx_flat, gate, h2, w_gu, w_down)


COMBINE_WINDOW_LOG2 = 5
COMBINE_WINDOW = 1 << COMBINE_WINDOW_LOG2


def _combine_kernel(idx_ref, off_ref, num_ref, ye_hbm, x_ref, g_ref, gate_ref, o_ref, stage, extra, acc, sem,
                    *, cap, ne, nbk):
    b = pl.program_id(0)
    slot = b % 2
    tb, d = x_ref.shape
    nq = d // LANES
    win = COMBINE_WINDOW

    def window_start(off):
        return jnp.minimum(off, cap - win)

    def start_windows(bb, slt):
        for e in range(ne):
            s0 = window_start(off_ref[e * nbk + bb])
            pltpu.make_async_copy(ye_hbm.at[pl.ds(e * cap + s0, win)], stage.at[slt, pl.ds(e * win, win)],
                                  sem.at[slt]).start()

    @pl.when(b == 0)
    def _():
        start_windows(b, slot)

    @pl.when(b + 1 < nbk)
    def _():
        start_windows(b + 1, 1 - slot)

    pltpu.make_async_copy(ye_hbm.at[pl.ds(0, ne * win)], stage.at[slot], sem.at[slot]).wait()

    acc[...] = jnp.zeros_like(acc)
    for e in range(ne):
        off = off_ref[e * nbk + b]
        n = num_ref[e * nbk + b]
        s0 = window_start(off)
        n_win = jnp.minimum(n, s0 + win - off)

        def add_row(j, carry, e=e, s0=s0):
            t = idx_ref[e * cap + j] - b * tb
            acc[t] = acc[t] + stage[slot, e * win + (j - s0)]
            return carry

        lax.fori_loop(off, off + n_win, add_row, 0)

        def overflow(c, carry, e=e, off=off, n=n, n_win=n_win):
            j0 = off + n_win + c * win
            s1 = window_start(j0)
            cp = pltpu.make_async_copy(ye_hbm.at[pl.ds(e * cap + s1, win)], extra, sem.at[2])
            cp.start()
            cp.wait()

            def add_extra(j, carry2):
                t = idx_ref[e * cap + j] - b * tb
                acc[t] = acc[t] + extra[j - s1]
                return carry2

            lax.fori_loop(j0, jnp.minimum(j0 + win, off + n), add_extra, 0)
            return carry

        lax.fori_loop(0, lax.shift_right_logical(n - n_win + win - 1, COMBINE_WINDOW_LOG2), overflow, 0)

    def pieces(k):
        return jnp.transpose(acc[:, k * SUBLANES:(k + 1) * SUBLANES, :], (1, 0, 2))

    sq = jnp.zeros((tb, LANES), f32)
    for k in range(nq // SUBLANES):
        p8 = pieces(k)
        for i in range(SUBLANES):
            sq = sq + p8[i] * p8[i]
    rinv = lax.rsqrt(jnp.sum(sq, axis=-1, keepdims=True) * (1.0 / d) + EPS)
    for k in range(nq // SUBLANES):
        p8 = pieces(k)
        for i in range(SUBLANES):
            q = k * SUBLANES + i
            sl = slice(q * LANES, (q + 1) * LANES)
            o_ref[:, sl] = x_ref[:, sl] + gate_ref[:, sl] * ((p8[i] * rinv) * g_ref[:, sl])


def moe_combine(idx_flat, off, num, ye, x, g, gate, *, cap):
    T, d = x.shape
    nq = d // LANES
    tb = LANES
    nbk = T // tb
    ne = off.shape[0] // nbk
    win = COMBINE_WINDOW
    assert cap >= win
    vec = pl.BlockSpec((1, d), lambda i, *_: (0, 0))
    return pl.pallas_call(
        functools.partial(_combine_kernel, cap=cap, ne=ne, nbk=nbk),
        grid_spec=pltpu.PrefetchScalarGridSpec(
            num_scalar_prefetch=3,
            grid=(nbk,),
            in_specs=[pl.BlockSpec(memory_space=pl.ANY),
                      pl.BlockSpec((tb, d), lambda i, *_: (i, 0)), vec, vec],
            out_specs=pl.BlockSpec((tb, d), lambda i, *_: (i, 0)),
            scratch_shapes=[pltpu.VMEM((2, ne * win, nq, LANES), f32), pltpu.VMEM((win, nq, LANES), f32),
                            pltpu.VMEM((tb, nq, LANES), f32), pltpu.SemaphoreType.DMA((3,))]),
        out_shape=jax.ShapeDtypeStruct((T, d), f32),
        compiler_params=_cparams("arbitrary"),
        name="moe_combine",
    )(idx_flat, off, num, ye, x, g, gate)


def moe_block(x, g_in, shift, scale, g_out, gate, p):
    T = x.shape[0]
    cap = EC_CAPACITY * T // N_EXPERTS
    nbk = T // LANES
    h3, aff_t = moe_router(x, g_in, shift, scale, p["router_wt"])
    idx, gates, off, num = moe_select(aff_t, cap)
    idx_flat = idx.reshape(-1)
    ye = moe_experts(idx_flat, gates, h3, p["w_gu"], p["w_down"], rows=min(cap, 512))
    return moe_combine(idx_flat, off[:, :nbk, 0].reshape(-1), num[:, :nbk, 0].reshape(-1), ye, x, g_out, gate, cap=cap)


def _conv_silu_kernel(x_ref, xp_ref, xn_ref, cw_ref, cb_ref, o_ref, ext_s, *, nt, tt):
    ti = pl.program_id(0)
    ext_s[0:8, :] = jnp.where(ti == 0, 0.0, xp_ref[...])
    ext_s[8:8 + tt, :] = x_ref[...]
    ext_s[8 + tt:16 + tt, :] = jnp.where(ti == nt - 1, 0.0, xn_ref[...])
    acc = cb_ref[...]
    for k in range(CONV_W):
        acc = acc + ext_s[pl.ds(6 + k, tt), :] * cw_ref[k:k + 1, :]
    o_ref[...] = _silu(acc)


def conv_silu(zx, conv_w, conv_b, col0, *, tt=512, tc=1024):
    T = zx.shape[0]
    n = conv_w.shape[1]
    tt = _row_tile(T, tt)
    nt = T // tt
    tb = tt // 8
    cb0 = col0 // tc
    return pl.pallas_call(
        functools.partial(_conv_silu_kernel, nt=nt, tt=tt),
        grid=(nt, n // tc),
        in_specs=[pl.BlockSpec((tt, tc), lambda t, j: (t, cb0 + j)),
                  pl.BlockSpec((8, tc), lambda t, j: (jnp.maximum(t * tb - 1, 0), cb0 + j)),
                  pl.BlockSpec((8, tc), lambda t, j: (jnp.minimum((t + 1) * tb, T // 8 - 1), cb0 + j)),
                  pl.BlockSpec((CONV_W, tc), lambda t, j: (0, j)),
                  pl.BlockSpec((1, tc), lambda t, j: (0, j))],
        out_specs=pl.BlockSpec((tt, tc), lambda t, j: (t, j)),
        out_shape=jax.ShapeDtypeStruct((T, n), f32),
        scratch_shapes=[pltpu.VMEM((tt + 16, tc), f32)],
        compiler_params=_cparams("parallel", "parallel"),
        name="conv_silu",
    )(zx, zx, zx, conv_w, conv_b)


def _ssd_prep_kernel(raw_ref, bias_ref, alog_ref, dt_ref, ac_ref, act_ref):
    L = SSD_CHUNK
    nh = raw_ref.shape[1] // 2
    hg = nh // SSD_GROUPS
    dt = _softplus(raw_ref[...] + bias_ref[...])
    dta = dt * (-jnp.exp(alog_ref[...]))
    r = lax.broadcasted_iota(i32, (L, L), 0)
    c = lax.broadcasted_iota(i32, (L, L), 1)
    hi = lax.Precision.HIGHEST
    acs = [jnp.dot((c <= r).astype(f32), dta[:, :nh], precision=hi, preferred_element_type=f32),
           jnp.dot((c >= r).astype(f32), dta[:, nh:], precision=hi, preferred_element_type=f32)]
    for d in range(2):
        act = acs[d].T
        for g in range(SSD_GROUPS):
            dt_ref[d, g] = dt[:, d * nh + g * hg:d * nh + (g + 1) * hg]
            ac_ref[d, g] = acs[d][:, g * hg:(g + 1) * hg]
            act_ref[d, g] = act[g * hg:(g + 1) * hg, :]


def ssd_prep(dt_raw, dt_bias, a_log):
    T, nh2 = dt_raw.shape
    nh = nh2 // 2
    hg = nh // SSD_GROUPS
    nc = T // SSD_CHUNK
    vec = pl.BlockSpec((1, nh2), lambda c: (0, 0))
    return pl.pallas_call(
        _ssd_prep_kernel,
        grid=(nc,),
        in_specs=[pl.BlockSpec((SSD_CHUNK, nh2), lambda c: (c, 0)), vec, vec],
        out_specs=[pl.BlockSpec((2, SSD_GROUPS, SSD_CHUNK, hg), lambda c: (0, 0, c, 0)),
                   pl.BlockSpec((2, SSD_GROUPS, SSD_CHUNK, hg), lambda c: (0, 0, c, 0)),
                   pl.BlockSpec((2, SSD_GROUPS, None, hg, SSD_CHUNK), lambda c: (0, 0, c, 0, 0))],
        out_shape=[jax.ShapeDtypeStruct((2, SSD_GROUPS, T, hg), f32),
                   jax.ShapeDtypeStruct((2, SSD_GROUPS, T, hg), f32),
                   jax.ShapeDtypeStruct((2, SSD_GROUPS, nc, hg, SSD_CHUNK), f32)],
        compiler_params=_cparams("parallel"),
        name="ssd_prep",
    )(dt_raw, dt_bias, a_log)


def _ssd_scan_kernel(*refs, reverse, final, hg):
    if final:
        (xs_ref, b_ref, c_ref, dt_ref, ac_ref, act_ref, s0_ref, yf_ref, z_ref, dsk_ref, ng_ref,
         out_ref, st_ref, yz_s) = refs
    else:
        (xs_ref, b_ref, c_ref, dt_ref, ac_ref, act_ref, s0_ref, out_ref, st_ref) = refs
    L = SSD_CHUNK
    P = SSD_HEADDIM
    ci = pl.program_id(0)
    g = pl.program_id(1)

    @pl.when(ci == 0)
    def _():
        st_ref[g] = s0_ref[g]

    xs = xs_ref[...]
    dt16 = dt_ref[...]
    ac16 = ac_ref[...]
    act = act_ref[...]
    expand = (lax.broadcasted_iota(i32, (hg, hg * P), 0)
              == lax.broadcasted_iota(i32, (hg, hg * P), 1) // P).astype(f32)
    dtx = jnp.dot(dt16, expand, precision=lax.Precision.HIGHEST, preferred_element_type=f32)
    xq = (xs * dtx).astype(bf16)
    bm = b_ref[...]
    cm = c_ref[...]
    cb = lax.dot_general(cm.astype(bf16), bm.astype(bf16), (((1,), (1,)), ((), ())), preferred_element_type=f32)
    bt = bm.T
    li = lax.broadcasted_iota(i32, (L, L), 0)
    si = lax.broadcasted_iota(i32, (L, L), 1)
    mask = (li <= si) if reverse else (li >= si)
    last = 0 if reverse else L - 1
    st = st_ref[g]
    stb = st.astype(bf16)
    ys = []
    new_st = []
    for hh in range(hg):
        hs = slice(hh * P, (hh + 1) * P)
        col = ac16[:, hh:hh + 1]
        row = act[hh:hh + 1, :]
        gm = (jnp.exp(jnp.where(mask, col - row, -jnp.inf)) * cb).astype(bf16)
        ce = (cm * jnp.exp(col)).astype(bf16)
        xqh = xq[:, hs]
        ys.append(jnp.dot(jnp.concatenate([gm, ce], axis=1), jnp.concatenate([xqh, stb[:, hs]], axis=0),
                          preferred_element_type=f32))
        tot = row[:, last:last + 1]
        bdt = (bt * jnp.exp(tot - row)).astype(bf16)
        new_st.append(st[:, hs] * jnp.exp(tot) + jnp.dot(bdt, xqh, preferred_element_type=f32))
    y = jnp.concatenate(ys, axis=1)
    st_ref[g] = jnp.concatenate(new_st, axis=1)
    if not final:
        out_ref[...] = y
        return
    yz_s[g] = (yf_ref[...] + y + dsk_ref[...] * xs) * _silu(z_ref[...])

    @pl.when(g == SSD_GROUPS - 1)
    def _():
        w = hg * P
        ss = None
        for k in range(SSD_GROUPS):
            v = yz_s[k]
            s = jnp.sum(v * v, axis=-1, keepdims=True)
            ss = s if ss is None else ss + s
        rinv = lax.rsqrt(ss * (1.0 / (w * SSD_GROUPS)) + EPS)
        for k in range(SSD_GROUPS):
            out_ref[:, k * w:(k + 1) * w] = ((yz_s[k] * rinv) * ng_ref[:, k * w:(k + 1) * w]).astype(out_ref.dtype)


def ssd_scan(xbc, dtg, acg, act, s0, fin=None, *, reverse):
    T = xbc.shape[0]
    L = SSD_CHUNK
    nc = T // L
    hg = dtg.shape[-1]
    w = hg * SSD_HEADDIM
    d_inner = w * SSD_GROUPS
    nb0 = d_inner // SSD_STATE
    final = fin is not None
    cix = (lambda c: nc - 1 - c) if reverse else (lambda c: c)
    full_state = pl.BlockSpec(s0.shape, lambda c, g: (0, 0, 0))
    in_specs = [
        pl.BlockSpec((L, w), lambda c, g: (cix(c), g)),
        pl.BlockSpec((L, SSD_STATE), lambda c, g: (cix(c), nb0 + g)),
        pl.BlockSpec((L, SSD_STATE), lambda c, g: (cix(c), nb0 + SSD_GROUPS + g)),
        pl.BlockSpec((None, L, hg), lambda c, g: (g, cix(c), 0)),
        pl.BlockSpec((None, L, hg), lambda c, g: (g, cix(c), 0)),
        pl.BlockSpec((None, None, hg, L), lambda c, g: (g, cix(c), 0, 0)),
        full_state,
    ]
    args = [xbc, xbc, xbc, dtg, acg, act, s0]
    scratch = []
    if final:
        yf, zx, dsk, ng = fin
        in_specs += [pl.BlockSpec((L, w), lambda c, g: (cix(c), g)),
                     pl.BlockSpec((L, w), lambda c, g: (cix(c), g)),
                     pl.BlockSpec((1, w), lambda c, g: (0, g)),
                     pl.BlockSpec((1, d_inner), lambda c, g: (0, 0))]
        args += [yf, zx, dsk, ng]
        out_spec = pl.BlockSpec((L, d_inner), lambda c, g: (cix(c), 0))
        out_shape = jax.ShapeDtypeStruct((T, d_inner), bf16)
        scratch = [pltpu.VMEM((SSD_GROUPS, L, w), f32)]
    else:
        out_spec = pl.BlockSpec((L, w), lambda c, g: (cix(c), g))
        out_shape = jax.ShapeDtypeStruct((T, d_inner), f32)
    return pl.pallas_call(
        functools.partial(_ssd_scan_kernel, reverse=reverse, final=final, hg=hg),
        grid=(nc, SSD_GROUPS),
        in_specs=in_specs,
        out_specs=[out_spec, full_state],
        out_shape=[out_shape, jax.ShapeDtypeStruct(s0.shape, f32)],
        scratch_shapes=scratch,
        compiler_params=_cparams("arbitrary", "arbitrary"),
        name="ssd_scan_bwd" if reverse else "ssd_scan_fwd",
    )(*args)


def ssd_mixer(h_args_l, h_args_c, p):
    d_inner = p["norm_g"].shape[1]
    hg = d_inner // SSD_HEADDIM // SSD_GROUPS
    s_zero = jnp.zeros((SSD_GROUPS, SSD_STATE, hg * SSD_HEADDIM), f32)

    def prep(h_args):
        zx = norm_mod_matmul(*h_args, p["w_zx"])
        dt_raw = norm_mod_matmul(*h_args, p["w_dt"], tn=p["w_dt"].shape[1])
        xbc = conv_silu(zx, p["conv_w"], p["conv_b"], d_inner)
        return (zx, xbc) + tuple(ssd_prep(dt_raw, p["dt_bias"], p["a_log"]))

    zx_c, xbc_c, dt_c, ac_c, act_c = prep(h_args_c)
    zx_l, xbc_l, dt_l, ac_l, act_l = prep(h_args_l)
    yf_c, s_c = ssd_scan(xbc_c, dt_c[0], ac_c[0], act_c[0], s_zero, reverse=False)
    yf_l, _ = ssd_scan(xbc_l, dt_l[0], ac_l[0], act_l[0], s_c, reverse=False)
    y_c, s_c = ssd_scan(xbc_c, dt_c[1], ac_c[1], act_c[1], s_zero, (yf_c, zx_c, p["d_skip"], p["norm_g"]), reverse=True)
    y_l, _ = ssd_scan(xbc_l, dt_l[1], ac_l[1], act_l[1], s_c, (yf_l, zx_l, p["d_skip"], p["norm_g"]), reverse=True)
    return y_l, y_c


def _rope_table_kernel(cos_ref, sin_ref, *, tt):
    half = HEAD_DIM // 2
    t = lax.broadcasted_iota(i32, (tt, HEAD_DIM), 0) + pl.program_id(0) * tt
    lane = lax.broadcasted_iota(i32, (tt, HEAD_DIM), 1)
    pos = jnp.where(lane < half, t // GRID_W, t % GRID_W).astype(f32)
    k = (lane % (half // 2)).astype(f32)
    inv_freq = jnp.exp(k * (-2.0 / half * math.log(ROPE_BASE)))
    ang = pos * inv_freq
    cos_ref[...] = jnp.cos(ang)
    sin_ref[...] = jnp.where(lane % half < half // 2, -1.0, 1.0) * jnp.sin(ang)


def rope_tables(T, *, tt=512):
    tt = _row_tile(T, tt)
    spec = pl.BlockSpec((tt, HEAD_DIM), lambda i: (i, 0))
    return pl.pallas_call(
        functools.partial(_rope_table_kernel, tt=tt),
        grid=(T // tt,),
        in_specs=[],
        out_specs=[spec, spec],
        out_shape=[jax.ShapeDtypeStruct((T, HEAD_DIM), f32)] * 2,
        compiler_params=_cparams("parallel"),
        name="rope_tables",
    )()


def _rope(x, cos, sin):
    n, w = x.shape
    reps = w // HEAD_DIM
    if reps > 1:
        cos = jnp.concatenate([cos] * reps, axis=1)
        sin = jnp.concatenate([sin] * reps, axis=1)
    q = HEAD_DIM // 4
    lane = lax.broadcasted_iota(i32, (n, w), 1)
    partner = jnp.where(lane % (2 * q) < q, pltpu.roll(x, w - q, axis=1), pltpu.roll(x, q, axis=1))
    return x * cos + partner * sin


def _attn_kernel(*refs, band):
    if band:
        (q_ref, kp_ref, ko_ref, kn_ref, vp_ref, vo_ref, vn_ref, cp_ref, co_ref, cn_ref, sp_ref, so_ref, sn_ref,
         kc_ref, vc_ref, sink_ref, bias_ref, o_ref) = refs
    else:
        q_ref, kc_ref, vc_ref, sink_ref, o_ref = refs
    nrow = q_ref.shape[0]
    q = q_ref[...]
    if band:
        q = _rope(q, co_ref[...], so_ref[...])
    qs = jnp.concatenate([q[:, j * HEAD_DIM:(j + 1) * HEAD_DIM] for j in range(ATTN_GROUP)], axis=0).astype(bf16)
    kc = kc_ref[...].astype(bf16)
    vc = vc_ref[...].astype(bf16)
    nt = (((1,), (1,)), ((), ()))
    scale = HEAD_DIM ** -0.5
    sink = sink_ref[...]
    s_c = lax.dot_general(qs, kc, nt, preferred_element_type=f32) * scale
    m = jnp.maximum(jnp.max(s_c, axis=1, keepdims=True), sink)
    if band:
        kb = jnp.concatenate([_rope(kp_ref[...], cp_ref[...], sp_ref[...]),
                              _rope(ko_ref[...], co_ref[...], so_ref[...]),
                              _rope(kn_ref[...], cn_ref[...], sn_ref[...])], axis=0).astype(bf16)
        vb = jnp.concatenate([vp_ref[...], vo_ref[...], vn_ref[...]], axis=0).astype(bf16)
        s_b = lax.dot_general(qs, kb, nt, preferred_element_type=f32) * scale \
            + jnp.concatenate([bias_ref[...]] * ATTN_GROUP, axis=0)
        m =jnp.maximum(m, jnp.max(s_b, axis=1, keepdims=True))
        p_b = jnp.exp(s_b - m)
    p_c = jnp.exp(s_c - m)
    den = jnp.sum(p_c, axis=1, keepdims=True) + jnp.exp(sink - m)
    if band:
        den = den + jnp.sum(p_b, axis=1, keepdims=True)
    rden = 1.0 / den
    o = jnp.dot((p_c * rden).astype(bf16), vc, preferred_element_type=f32)
    if band:
        o = o + jnp.dot((p_b * rden).astype(bf16), vb, preferred_element_type=f32)
    o_ref[...] = jnp.concatenate([o[j * nrow:(j + 1) * nrow, :] for j in range(ATTN_GROUP)], axis=1).astype(o_ref.dtype)


def window_attention(qkv_l, qkv_c, sink_rows, cos, sin):
    T = qkv_l.shape[0]
    Lc = qkv_c.shape[0]
    nkv = qkv_l.shape[1] // HEAD_DIM // (ATTN_GROUP + 2)
    nh = nkv * ATTN_GROUP
    k0, v0 = nh, nh + nkv
    gw = ATTN_GROUP * HEAD_DIM
    qb = ATTN_QBLOCKS
    blk = qb * WINDOW
    nq = T // blk
    nw = T // WINDOW
    rows = [(WINDOW, lambda i: jnp.maximum(qb * i - 1, 0)), (blk, lambda i: i),
            (WINDOW, lambda i: jnp.minimum(qb * (i + 1), nw - 1))]
    kv = lambda c0: [pl.BlockSpec((r, HEAD_DIM), (lambda g, i, f=f, c0=c0: (f(i), c0 + g))) for r, f in rows]
    tab = [pl.BlockSpec((r, HEAD_DIM), (lambda g, i, f=f: (f(i), 0))) for r, f in rows]
    ctx_specs = lambda: [pl.BlockSpec((Lc, HEAD_DIM), lambda g, i: (0, k0 + g)),
                         pl.BlockSpec((Lc, HEAD_DIM), lambda g, i: (0, v0 + g))]
    r = jnp.arange(blk)[:, None] + WINDOW
    c = jnp.arange(blk + 2 * WINDOW)[None, :]

    def bias_of(i):
        k_abs = c + (i * blk - WINDOW)
        return jnp.where((jnp.abs(r - c) <= WINDOW) & (k_abs >= 0) & (k_abs < T), 0.0, -jnp.inf).astype(f32)

    bias = jnp.stack([bias_of(0), bias_of(min(1, nq - 1)), bias_of(nq - 1)])
    which = lambda i: jnp.where(i == 0, 0, jnp.where(i == nq - 1, 2, 1))
    o_l = pl.pallas_call(
        functools.partial(_attn_kernel, band=True),
        grid=(nkv, nq),
        in_specs=[pl.BlockSpec((blk, gw), lambda g, i: (i, g))] + kv(k0) + kv(v0) + tab + tab + ctx_specs()
        + [pl.BlockSpec((None, ATTN_GROUP * blk, 1), lambda g, i: (g, 0, 0)),
           pl.BlockSpec((None, blk, blk + 2 * WINDOW), lambda g, i: (which(i), 0, 0))],
        out_specs=pl.BlockSpec((blk, gw), lambda g, i: (i, g)),
        out_shape=jax.ShapeDtypeStruct((T, nh * HEAD_DIM), bf16),
        compiler_params=_cparams("parallel", "parallel"),
        name="window_attention",
    )(qkv_l, *([qkv_l] * 6), cos, cos, cos, sin, sin, sin, qkv_c, qkv_c, sink_rows[0], bias)
    o_c = pl.pallas_call(
        functools.partial(_attn_kernel, band=False),
        grid=(nkv, 1),
        in_specs=[pl.BlockSpec((Lc, gw), lambda g, i: (0, g))] + ctx_specs()
        + [pl.BlockSpec((None, ATTN_GROUP * Lc, 1), lambda g, i: (g, 0, 0))],
        out_specs=pl.BlockSpec((Lc, gw), lambda g, i: (0, g)),
        out_shape=jax.ShapeDtypeStruct((Lc, nh * HEAD_DIM), bf16),
        compiler_params=_cparams("parallel", "parallel"),
        name="context_attention",
    )(qkv_c, qkv_c, qkv_c, sink_rows[1])
    return o_l, o_c


def _adaln_kernel(c_ref, dn_ref, up_ref, b_ref, o_ref):
    hi = lax.Precision.HIGHEST
    t = jnp.dot(jax.nn.silu(c_ref[...]), dn_ref[...], precision=hi, preferred_element_type=f32)
    o_ref[...] = jnp.dot(t, up_ref[...], precision=hi, preferred_element_type=f32) + b_ref[...]


def adaln(cond, w_down, w_up, b, *, tn=2048):
    depth, d, r = w_down.shape
    n = w_up.shape[2]
    rows = cond.shape[0]
    return pl.pallas_call(
        _adaln_kernel,
        grid=(depth, n // tn),
        in_specs=[pl.BlockSpec((rows, d), lambda l, j: (0, 0)),
                  pl.BlockSpec((None, d, r), lambda l, j: (l, 0, 0)),
                  pl.BlockSpec((None, r, tn), lambda l, j: (l, 0, j)),
                  pl.BlockSpec((None, 1, tn), lambda l, j: (l, 0, j))],
        out_specs=pl.BlockSpec((None, rows, tn), lambda l, j: (l, 0, j)),
        out_shape=jax.ShapeDtypeStruct((depth, rows, n), f32),
        compiler_params=_cparams("parallel", "parallel"),
        name="adaln",
    )(cond, w_down, w_up, b.reshape(depth, 1, n))


def kernel(x, c, ctx, c_ctx, ada_down, ada_up, ada_b, norm_g, router_w, moe_w_gate, moe_w_up, moe_w_down,
           rg_w_in, rg_conv_w, rg_conv_b, rg_gate_w, rg_gate_b, rg_lambda, rg_w_out,
           ssd_w_in, ssd_conv_w, ssd_conv_b, ssd_dt_bias, ssd_a_log, ssd_d, ssd_norm_g, ssd_w_out,
           attn_w_qkv, attn_sink, attn_w_out):
    assert x.shape[0] == 1 and ctx.shape[0] == 1 and c.shape[0] == 1
    depth = ada_down.shape[0]
    T, d = x.shape[1], x.shape[2]
    Lc = ctx.shape[1]
    xl, xc = x[0], ctx[0]
    cond = jnp.concatenate([c, c_ctx[None, :], jnp.zeros((SUBLANES - 2, d), f32)], axis=0)
    mods = adaln(cond, ada_down, ada_up, ada_b)
    row = lambda v: v.reshape(1, -1)

    for i in range(depth):
        kind, slot = i % N_MIXERS, i // N_MIXERS
        ml = [mods[i, 0:1, k * d:(k + 1) * d] for k in range(N_MOD)]
        mc = [mods[i, 1:2, k * d:(k + 1) * d] for k in range(N_MOD)]
        g = [row(norm_g[i, k]) for k in range(4)]
        last = i == depth - 1
        in_l = (xl, g[0], ml[0], ml[1])
        in_c = (xc, g[0], mc[0], mc[1])
        if kind == 0:
            p = dict(w_in=to_bf16(rg_w_in, layer=slot), conv_w=rg_conv_w[slot], conv_b=row(rg_conv_b[slot]),
                     gate_w=to_bf16(rg_gate_w, layer=slot), gate_b=rg_gate_b[slot],
                     lam=rg_lambda[slot].reshape(2, 1, -1))
            y_l, y_c = rglru_mixer(in_l, in_c, p)
            w_out = to_bf16(rg_w_out, layer=slot)
        elif kind == 1:
            d_inner = ssd_norm_g.shape[1]
            n_zx = d_inner + ssd_conv_w.shape[2]
            p = dict(w_zx=to_bf16(ssd_w_in, layer=slot, col0=0, ncols=n_zx),
                     w_dt=to_bf16(ssd_w_in, layer=slot, col0=n_zx, ncols=ssd_w_in.shape[2] - n_zx),
                     conv_w=ssd_conv_w[slot], conv_b=row(ssd_conv_b[slot]),
                     dt_bias=row(ssd_dt_bias[slot]), a_log=row(ssd_a_log[slot]),
                     d_skip=row(jnp.repeat(ssd_d[slot], SSD_HEADDIM)), norm_g=row(ssd_norm_g[slot]))
            y_l, y_c = ssd_mixer(in_l, in_c, p)
            w_out = to_bf16(ssd_w_out, layer=slot)
        else:
            w_qkv = to_bf16(attn_w_qkv, layer=slot)
            qkv_l = norm_mod_matmul(*in_l, w_qkv)
            qkv_c = norm_mod_matmul(*in_c, w_qkv)
            sk = attn_sink[slot].reshape(-1, ATTN_GROUP, 1, 1)
            sink_rows = [jnp.broadcast_to(sk, sk.shape[:2] + (n, 1)).reshape(sk.shape[0], ATTN_GROUP * n, 1)
                         for n in (ATTN_QBLOCKS * WINDOW, Lc)]
            cos, sin = rope_tables(T)
            y_l, y_c = window_attention(qkv_l, qkv_c, sink_rows, cos, sin)
            w_out = to_bf16(attn_w_out, layer=slot)
        mp = dict(router_wt=router_w[i].T, w_gu=to_bf16(moe_w_gate, moe_w_up, layer=i),
                  w_down=to_bf16(moe_w_down, layer=i))
        xl = matmul_norm_res(y_l, w_out, g[1], ml[2], xl)
        xl = moe_block(xl, g[2], ml[3], ml[4], g[3], ml[5], mp)
        if not last:
            xc = matmul_norm_res(y_c, w_out, g[1], mc[2], xc)
            xc = moe_block(xc, g[2], mc[3], mc[4], g[3], mc[5], mp)
    return xl[None]
```
